```python
import jax
import jax.numpy as jnp
from jax import lax
import numpy as np

D_MODEL = 1024
BATCH = 16
SEQ = 256
DEPTH = 2
DEC_BATCH = 8
DEC_SEQ = 4096
PAST_LEN = 512

GRID_W = 64
HEAD_DIM = 64
ATT_HEADS = 8
ATT_KV_HEADS = 2
ATT_GROUP = ATT_HEADS // ATT_KV_HEADS
ATT_WIDTH = ATT_HEADS * HEAD_DIM
KV_WIDTH = ATT_KV_HEADS * HEAD_DIM
M_HEADS = 4
M_DIM = 64
M_WIDTH = M_HEADS * M_DIM
G_GROUPS = 4
G_WIDTH = D_MODEL - ATT_WIDTH - M_WIDTH
CHUNK = 128
Q_BLOCK = 128
D_FF = 4 * D_MODEL
ROPE_BASE = 10000.0
EPS = 1e-6
DEEPNORM_ALPHA = (2 * DEPTH) ** 0.25
DEEPNORM_BETA = (8 * DEPTH) ** -0.25
PROJ_SIZES = (ATT_WIDTH, KV_WIDTH, KV_WIDTH, M_WIDTH, M_WIDTH, M_WIDTH, M_WIDTH, 4 * M_HEADS, G_WIDTH, G_WIDTH)
PROJ_WIDTH = ATT_WIDTH + 2 * KV_WIDTH + 4 * M_WIDTH + 4 * M_HEADS + 2 * G_WIDTH

kernel_name = "hybrid_diffusion_prefix_step"


def layer_norm(x, g, b):
    xf = x.astype(jnp.float32)
    mu = jnp.mean(xf, axis=-1, keepdims=True)
    var = jnp.mean(jnp.square(xf - mu), axis=-1, keepdims=True)
    y = (xf - mu) * lax.rsqrt(var + EPS) * g.astype(jnp.float32) + b.astype(jnp.float32)
    return y.astype(x.dtype)


def rms_norm(x, g):
    xf = x.astype(jnp.float32)
    y = xf * lax.rsqrt(jnp.mean(jnp.square(xf), axis=-1, keepdims=True) + EPS) * g.astype(jnp.float32)
    return y.astype(x.dtype)


def rope_1d(x, ang):
    cos = jnp.cos(ang)[None, :, None, :]
    sin = jnp.sin(ang)[None, :, None, :]
    xa, xb = jnp.split(x.astype(jnp.float32), 2, axis=-1)
    return jnp.concatenate([xa * cos - xb * sin, xa * sin + xb * cos], axis=-1).astype(x.dtype)


def axial_rope(x, rows):
    r, cidx = jnp.meshgrid(jnp.arange(rows, dtype=jnp.float32), jnp.arange(GRID_W, dtype=jnp.float32), indexing='ij')
    n_freq = HEAD_DIM // 4
    inv = ROPE_BASE ** (-jnp.arange(n_freq, dtype=jnp.float32) / n_freq)
    ang_r = r.reshape(-1)[:, None] * inv[None, :]
    ang_c = cidx.reshape(-1)[:, None] * inv[None, :]
    half = HEAD_DIM // 2
    return jnp.concatenate([rope_1d(x[..., :half], ang_r), rope_1d(x[..., half:], ang_c)], axis=-1)


def blocked_attention(q, k, v):
    B, T = q.shape[0], q.shape[1]
    nb = T // Q_BLOCK
    qb = q.reshape(B, nb, Q_BLOCK, ATT_KV_HEADS, ATT_GROUP, HEAD_DIM).transpose(1, 0, 2, 3, 4, 5)
    scale = HEAD_DIM ** -0.5

    def one_block(qblk):
        s = jnp.einsum('bqkgd,bskd->bkgqs', qblk, k).astype(jnp.float32) * scale
        p = jax.nn.softmax(s, axis=-1).astype(v.dtype)
        return jnp.einsum('bkgqs,bskd->bqkgd', p, v)

    o = lax.map(one_block, qb)
    return o.transpose(1, 0, 2, 3, 4, 5).reshape(B, T, ATT_WIDTH)


def mlstm_scan(q, k, v, ig, lf, state):
    B, T, H, D = q.shape
    nc = T // CHUNK
    to_c = lambda a: a.reshape(B, nc, CHUNK, H, D).transpose(1, 0, 3, 2, 4)
    to_cg = lambda a: a.reshape(B, nc, CHUNK, H).transpose(1, 0, 3, 2)
    mask = jnp.tril(jnp.ones((CHUNK, CHUNK), dtype=bool))

    def step(carry, inp):
        C, n, m = carry
        qc, kc, vc, ic, fc = inp
        b = jnp.cumsum(fc, axis=-1)
        logw = jnp.where(mask, b[..., :, None] - b[..., None, :] + ic[..., None, :], -jnp.inf)
        m_t = jnp.maximum(b + m[..., None], jnp.max(logw, axis=-1))
        w = jnp.exp(logw - m_t[..., None])
        s = jnp.einsum('bhtd,bhsd->bhts', qc, kc) * w
        inter = jnp.exp(b + m[..., None] - m_t)
        num = jnp.einsum('bhts,bhse->bhte', s, vc) + inter[..., None] * jnp.einsum('bhtd,bhde->bhte', qc, C)
        den = jnp.sum(s, axis=-1) + inter * jnp.einsum('bhtd,bhd->bht', qc, n)
        h = num / jnp.maximum(jnp.abs(den), jnp.exp(-m_t))[..., None]
        m_new = m_t[..., -1]
        decay = jnp.exp(b[..., -1] + m - m_new)
        wk = jnp.exp(b[..., -1:] - b + ic - m_new[..., None])
        C_new = decay[..., None, None] * C + jnp.einsum('bhs,bhsd,bhse->bhde', wk, kc, vc)
        n_new = decay[..., None] * n + jnp.einsum('bhs,bhsd->bhd', wk, kc)
        return (C_new, n_new, m_new), h

    state = tuple(s.astype(jnp.float32) for s in state)
    final, hs = lax.scan(step, state, (to_c(q), to_c(k), to_c(v), to_cg(ig), to_cg(lf)))
    return hs.transpose(1, 0, 3, 2, 4).reshape(B, T, H, D), final


def mlstm_bidir(q, k, v, o_pre, gates, state_f, state_b):
    B, T = q.shape[0], q.shape[1]
    flip = lambda a: jnp.flip(a, axis=1)
    h_f, st_f = mlstm_scan(q, k, v, gates[:, :, 0], jax.nn.log_sigmoid(gates[:, :, 1]), state_f)
    h_b, st_b = mlstm_scan(flip(q), flip(k), flip(v), flip(gates[:, :, 2]),
                           flip(jax.nn.log_sigmoid(gates[:, :, 3])), state_b)
    h = (h_f + flip(h_b)).reshape(B, T, M_WIDTH)
    return jax.nn.sigmoid(o_pre.astype(jnp.float32)) * h, st_f, st_b


def gmlp_spatial(u, v, ln_g, ln_b, ws, bs):
    B, T, _ = v.shape
    vn = layer_norm(v, ln_g, ln_b)
    vr = vn.reshape(B, T // CHUNK, CHUNK, G_GROUPS, G_WIDTH // G_GROUPS)
    s = jnp.einsum('gts,bcsgd->bctgd', ws, vr) + bs.T[None, None, :, :, None]
    return u * s.reshape(B, T, G_WIDTH)


def modulation(cv, w_mod, b_mod):
    return (jax.nn.silu(cv) @ w_mod + b_mod)[:, None, :]


def trunk_layer(x, mod, p, ctx=None):
    (w_in, q_g, k_g, gate_b, gln_g, gln_b, g_ws, g_bs, w_out, ln1_g, ln1_b, w_ff1, w_ff2, ln2_g, ln2_b) = p
    sh1, sc1, g1, sh2, sc2, g2 = jnp.split(mod, 6, axis=-1)
    B, T, _ = x.shape
    h = x * (1 + sc1) + sh1
    z = h @ w_in
    splits = np.cumsum(PROJ_SIZES)[:-1].tolist()
    qa, ka, va, qm, km, vm, om, gm, ug, vg = jnp.split(z, splits, axis=-1)
    qa = rms_norm(qa.reshape(B, T, ATT_HEADS, HEAD_DIM), q_g)
    ka = rms_norm(ka.reshape(B, T, ATT_KV_HEADS, HEAD_DIM), k_g)
    va = va.reshape(B, T, ATT_KV_HEADS, HEAD_DIM)
    qm = qm.reshape(B, T, M_HEADS, M_DIM).astype(jnp.float32)
    km = km.reshape(B, T, M_HEADS, M_DIM).astype(jnp.float32) * (M_DIM ** -0.5)
    vm = vm.reshape(B, T, M_HEADS, M_DIM).astype(jnp.float32)
    gates = gm.reshape(B, T, 4, M_HEADS).astype(jnp.float32) + gate_b.astype(jnp.float32)
    if ctx is None:
        q_att, k_all, v_all = qa, ka, va
        zero = (jnp.zeros((B, M_HEADS, M_DIM, M_DIM), jnp.float32),
                jnp.zeros((B, M_HEADS, M_DIM), jnp.float32),
                jnp.zeros((B, M_HEADS), jnp.float32))
        st_f, st_b = zero, zero
    else:
        k_c, v_c, st_f, st_b = ctx
        rows = T // GRID_W
        q_att = axial_rope(qa, rows)
        k_all = jnp.concatenate([k_c.astype(x.dtype), axial_rope(ka, rows)], axis=1)
        v_all = jnp.concatenate([v_c.astype(x.dtype), va], axis=1)
    att = blocked_attention(q_att, k_all, v_all)
    m_out, st_f_new, st_b_new = mlstm_bidir(qm, km, vm, om, gates, st_f, st_b)
    g_out = gmlp_spatial(ug, vg, gln_g, gln_b, g_ws, g_bs)
    y = jnp.concatenate([att, m_out.astype(x.dtype), g_out], axis=-1) @ w_out
    x = layer_norm(DEEPNORM_ALPHA * x + g1 * y, ln1_g, ln1_b)
    h2 = x * (1 + sc2) + sh2
    f = jnp.square(jax.nn.relu(h2 @ w_ff1)) @ w_ff2
    x = layer_norm(DEEPNORM_ALPHA * x + g2 * f, ln2_g, ln2_b)
    if ctx is None:
        return x, (ka, va, st_f_new, st_b_new)
    return x, None


def setup_inputs(seed: int = 0) -> dict:
    key = jax.random.key(seed)
    ks = jax.random.split(key, 32)
    nrm = lambda k, shape, s: jax.random.normal(k, shape, jnp.float32) * s
    D = D_MODEL
    gate_off = jnp.array([0.0, 3.0, 0.0, 3.0], jnp.float32)[None, :, None]
    return {
        "x_prompt": nrm(ks[0], (BATCH, SEQ, D), 1.0),
        "x_sample": nrm(ks[1], (DEC_BATCH, DEC_SEQ, D), 1.0),
        "c": nrm(ks[2], (DEC_BATCH, D), 1.0),
        "cache_attn_k": nrm(ks[3], (DEC_BATCH, DEPTH, PAST_LEN, ATT_KV_HEADS, HEAD_DIM), 1.0),
        "cache_attn_v": nrm(ks[4], (DEC_BATCH, DEPTH, PAST_LEN, ATT_KV_HEADS, HEAD_DIM), 1.0),
        "state_mlstm_C": nrm(ks[5], (DEC_BATCH, DEPTH, 2, M_HEADS, M_DIM, M_DIM), 0.1),
        "state_mlstm_n": nrm(ks[6], (DEC_BATCH, DEPTH, 2, M_HEADS, M_DIM), 0.1),
        "state_mlstm_m": nrm(ks[7], (DEC_BATCH, DEPTH, 2, M_HEADS), 0.5),
        "c_ctx": nrm(ks[8], (D,), 1.0),
        "w_mod": nrm(ks[9], (DEPTH, D, 6 * D), 0.5 * D ** -0.5),
        "b_mod": nrm(ks[10], (DEPTH, 6 * D), 0.1),
        "w_in": nrm(ks[11], (DEPTH, D, PROJ_WIDTH), D ** -0.5),
        "q_norm_g": 1.0 + nrm(ks[12], (DEPTH, HEAD_DIM), 0.1),
        "k_norm_g": 1.0 + nrm(ks[13], (DEPTH, HEAD_DIM), 0.1),
        "mlstm_gate_b": gate_off + nrm(ks[14], (DEPTH, 4, M_HEADS), 0.3),
        "gmlp_ln_g": 1.0 + nrm(ks[15], (DEPTH, G_WIDTH), 0.1),
        "gmlp_ln_b": nrm(ks[16], (DEPTH, G_WIDTH), 0.02),
        "gmlp_ws": nrm(ks[17], (DEPTH, G_GROUPS, CHUNK, CHUNK), CHUNK ** -0.5),
        "gmlp_bs": 1.0 + nrm(ks[18], (DEPTH, G_GROUPS, CHUNK), 0.1),
        "w_out": nrm(ks[19], (DEPTH, D, D), DEEPNORM_BETA * D ** -0.5),
        "ln1_g": 1.0 + nrm(ks[20], (DEPTH, D), 0.1),
        "ln1_b": nrm(ks[21], (DEPTH, D), 0.02),
        "w_ff1": nrm(ks[22], (DEPTH, D, D_FF), D ** -0.5),
        "w_ff2": nrm(ks[23], (DEPTH, D_FF, D), DEEPNORM_BETA * D_FF ** -0.5),
        "ln2_g": 1.0 + nrm(ks[24], (DEPTH, D), 0.1),
        "ln2_b": nrm(ks[25], (DEPTH, D), 0.02),
    }


def reference(x_prompt, x_sample, c, cache_attn_k, cache_attn_v, state_mlstm_C, state_mlstm_n, state_mlstm_m,
              c_ctx, w_mod, b_mod, w_in, q_norm_g, k_norm_g, mlstm_gate_b, gmlp_ln_g, gmlp_ln_b, gmlp_ws, gmlp_bs,
              w_out, ln1_g, ln1_b, w_ff1, w_ff2, ln2_g, ln2_b):
    xp, xs = x_prompt, x_sample
    ks_, vs_, Cs_, ns_, ms_ = [], [], [], [], []
    for l in range(DEPTH):
        p = (w_in[l], q_norm_g[l], k_norm_g[l], mlstm_gate_b[l], gmlp_ln_g[l], gmlp_ln_b[l], gmlp_ws[l],
             gmlp_bs[l], w_out[l], ln1_g[l], ln1_b[l], w_ff1[l], w_ff2[l], ln2_g[l], ln2_b[l])
        mod_ctx = modulation(c_ctx[None, :], w_mod[l], b_mod[l])
        xp, (k_l, v_l, sf, sb) = trunk_layer(xp, mod_ctx, p)
        ks_.append(k_l)
        vs_.append(v_l)
        Cs_.append(jnp.stack([sf[0], sb[0]], axis=1))
        ns_.append(jnp.stack([sf[1], sb[1]], axis=1))
        ms_.append(jnp.stack([sf[2], sb[2]], axis=1))
        mod_lat = modulation(c, w_mod[l], b_mod[l])
        ctx = (cache_attn_k[:, l], cache_attn_v[:, l],
               (state_mlstm_C[:, l, 0], state_mlstm_n[:, l, 0], state_mlstm_m[:, l, 0]),
               (state_mlstm_C[:, l, 1], state_mlstm_n[:, l, 1], state_mlstm_m[:, l, 1]))
        xs, _ = trunk_layer(xs, mod_lat, p, ctx)
    new_attn_k = jnp.stack(ks_, axis=1)
    new_attn_v = jnp.stack(vs_, axis=1)
    new_mlstm_C = jnp.stack(Cs_, axis=1)
    new_mlstm_n = jnp.stack(ns_, axis=1)
    new_mlstm_m = jnp.stack(ms_, axis=1)
    return (xp, xs, new_attn_k, new_attn_v, new_mlstm_C, new_mlstm_n, new_mlstm_m)
```

```python
import functools

import jax
import jax.numpy as jnp
from jax import lax
from jax.experimental import pallas as pl
from jax.experimental.pallas import tpu as pltpu

D_MODEL = 1024
DEPTH = 2
GRID_W = 64
HEAD_DIM = 64
ATT_HEADS = 8
ATT_KV_HEADS = 2
ATT_GROUP = ATT_HEADS // ATT_KV_HEADS
ATT_WIDTH = ATT_HEADS * HEAD_DIM
KV_WIDTH = ATT_KV_HEADS * HEAD_DIM
M_HEADS = 4
M_DIM = 64
M_WIDTH = M_HEADS * M_DIM
G_GROUPS = 4
G_WIDTH = D_MODEL - ATT_WIDTH - M_WIDTH
G_CH = G_WIDTH // G_GROUPS
CHUNK = 128
D_FF = 4 * D_MODEL
ROPE_BASE = 10000.0
EPS = 1e-6
DEEPNORM_ALPHA = (2 * DEPTH) ** 0.25
N_GATES = 4 * M_HEADS

LANES = 128
BF16_SUBLANES = 16
GATE_PAD = LANES
OFF_QA = 0
OFF_KA = OFF_QA + ATT_WIDTH
OFF_VA = OFF_KA + KV_WIDTH
OFF_QM = OFF_VA + KV_WIDTH
OFF_KM = OFF_QM + M_WIDTH
OFF_VM = OFF_KM + M_WIDTH
OFF_OM = OFF_VM + M_WIDTH
OFF_UG = OFF_OM + M_WIDTH
OFF_VG = OFF_UG + G_WIDTH
OFF_GM = OFF_VG + G_WIDTH
PROJ_PAD = OFF_GM + GATE_PAD
V_ROWS = HEAD_DIM + BF16_SUBLANES
VMEM_LIMIT = 56 * 1024 * 1024

F32 = jnp.float32
BF16 = jnp.bfloat16


def _dot(a, b):
    return jnp.dot(a, b, preferred_element_type=F32)


def _resident(shape):
    nd = len(shape)
    return pl.BlockSpec(shape, lambda *_: (0,) * nd, pipeline_mode=pl.Buffered(1))


def _sigmoid(x):
    return 1.0 / (1.0 + jnp.exp(-x))


def _log_sigmoid(x):
    return jnp.minimum(x, 0.0) - jnp.log1p(jnp.exp(-jnp.abs(x)))


def _layer_norm(x, g, b):
    mu = jnp.mean(x, axis=-1, keepdims=True)
    xc = x - mu
    var = jnp.mean(xc * xc, axis=-1, keepdims=True)
    return xc * lax.rsqrt(var + EPS) * g + b


def _mod_kernel(c_ref, w_ref, b_ref, o_ref):
    c = c_ref[...]
    s = (c * _sigmoid(c)).astype(BF16)
    o_ref[0] = _dot(s, w_ref[0].astype(BF16)) + b_ref[0]


def _modulation(cvec, w_mod, b_mod):
    n = cvec.shape[0]
    bn = 1536
    return pl.pallas_call(
        _mod_kernel,
        grid=(DEPTH, 6 * D_MODEL // bn),
        in_specs=[
            pl.BlockSpec((n, D_MODEL), lambda l, j: (0, 0)),
            pl.BlockSpec((1, D_MODEL, bn), lambda l, j: (l, 0, j)),
            pl.BlockSpec((1, 1, bn), lambda l, j: (l, 0, j)),
        ],
        out_specs=pl.BlockSpec((1, n, bn), lambda l, j: (l, 0, j)),
        out_shape=jax.ShapeDtypeStruct((DEPTH, n, 6 * D_MODEL), F32),
        compiler_params=pltpu.CompilerParams(
            dimension_semantics=("arbitrary", "arbitrary"), vmem_limit_bytes=VMEM_LIMIT),
        name="modulation",
    )(cvec, w_mod, b_mod.reshape(DEPTH, 1, 6 * D_MODEL))


def _norm_rope_heads(zT, gain, cos, sin, n_heads):
    outs = []
    for hd in range(n_heads):
        t = zT[hd * HEAD_DIM:(hd + 1) * HEAD_DIM, :]
        ms = jnp.mean(t * t, axis=0, keepdims=True)
        y = t * lax.rsqrt(ms + EPS) * gain
        if cos is not None:
            q = HEAD_DIM // 4
            sw = jnp.concatenate([y[q:2 * q], y[0:q], y[3 * q:4 * q], y[2 * q:3 * q]], axis=0)
            y = y * cos + sw * sin
        outs.append(y)
    return outs


def _inproj_kernel(*refs, rope, emit_kv32, tm):
    it = iter(refs)
    x_ref, mod_ref, w_ref, gq_ref, gk_ref = (next(it) for _ in range(5))
    cos_ref, sin_ref = (next(it), next(it)) if rope else (None, None)
    gb_ref, lng_ref, lnb_ref, ws_ref, bsx_ref = (next(it) for _ in range(5))
    qT_ref, k_ref, vT_ref, qm_ref, kmT_ref, vm_ref, so_ref, go_ref, gcol_ref, grow_ref = (
        next(it) for _ in range(10))
    k32_ref, v32_ref = (next(it), next(it)) if emit_kv32 else (None, None)

    x = x_ref[0]
    mod = mod_ref[0]
    sh1 = mod[:, 0:D_MODEL]
    sc1 = mod[:, D_MODEL:2 * D_MODEL]
    h = (x * (1.0 + sc1) + sh1).astype(BF16)

    cos = cos_ref[...] if rope else None
    sin = sin_ref[...] if rope else None

    zq = _dot(h, w_ref[:, OFF_QA:OFF_QA + ATT_WIDTH])
    q_heads = _norm_rope_heads(zq.T, gq_ref[...], cos, sin, ATT_HEADS)
    for hd in range(ATT_HEADS):
        qT_ref[0, hd] = q_heads[hd].astype(BF16)

    zk = _dot(h, w_ref[:, OFF_KA:OFF_KA + KV_WIDTH])
    k_heads = _norm_rope_heads(zk.T, gk_ref[...], cos, sin, ATT_KV_HEADS)
    kk = jnp.concatenate(k_heads, axis=0).T
    for hd in range(ATT_KV_HEADS):
        k_ref[0, hd] = kk[:, hd * HEAD_DIM:(hd + 1) * HEAD_DIM].astype(BF16)
    zv = _dot(h, w_ref[:, OFF_VA:OFF_VA + KV_WIDTH])
    vT = zv.T
    ones_rows = jnp.ones((BF16_SUBLANES, tm), BF16)
    for hd in range(ATT_KV_HEADS):
        vT_ref[0, hd] = jnp.concatenate(
            [vT[hd * HEAD_DIM:(hd + 1) * HEAD_DIM].astype(BF16), ones_rows], axis=0)
    if emit_kv32:
        k32_ref[0] = kk
        v32_ref[0] = zv

    zqm = _dot(h, w_ref[:, OFF_QM:OFF_QM + M_WIDTH])
    zkmT = (_dot(h, w_ref[:, OFF_KM:OFF_KM + M_WIDTH]) * (M_DIM ** -0.5)).T
    zvm = _dot(h, w_ref[:, OFF_VM:OFF_VM + M_WIDTH])
    for hd in range(M_HEADS):
        qm_ref[0, hd] = zqm[:, hd * M_DIM:(hd + 1) * M_DIM].astype(BF16)
        kmT_ref[0, hd] = zkmT[hd * M_DIM:(hd + 1) * M_DIM, :].astype(BF16)
        vm_ref[0, hd] = zvm[:, hd * M_DIM:(hd + 1) * M_DIM].astype(BF16)
    zo = _dot(h, w_ref[:, OFF_OM:OFF_OM + M_WIDTH])
    so_ref[0] = _sigmoid(zo).astype(BF16)

    zg = _dot(h, w_ref[:, OFF_GM:OFF_GM + GATE_PAD]) + gb_ref[...]
    lane = lax.broadcasted_iota(jnp.int32, zg.shape, 1)
    is_forget = (lane % (2 * M_HEADS)) >= M_HEADS
    gp = jnp.where(is_forget, _log_sigmoid(zg), zg)
    gcol_ref[0] = gp
    grow_ref[0] = gp.T[0:N_GATES, :]

    zu = _dot(h, w_ref[:, OFF_UG:OFF_UG + G_WIDTH])
    zvg = _dot(h, w_ref[:, OFF_VG:OFF_VG + G_WIDTH])
    vn = _layer_norm(zvg, lng_ref[...], lnb_ref[...])
    group = lax.broadcasted_iota(jnp.int32, (CHUNK, G_WIDTH), 1) // G_CH
    bsx = bsx_ref[...]
    s_chunks = []
    for j in range(tm // CHUNK):
        vc = vn[j * CHUNK:(j + 1) * CHUNK, :]
        acc = bsx
        for g in range(G_GROUPS):
            acc = acc + _dot(ws_ref[g], jnp.where(group == g, vc, 0.0).astype(BF16))
        s_chunks.append(acc)
    s = jnp.concatenate(s_chunks, axis=0) if len(s_chunks) > 1 else s_chunks[0]
    go_ref[0] = (zu * s).astype(BF16)


def _inproj(x, mod, mod_row, w_in, gq, gk, cos, sin, gate_b, lng, lnb, ws, bsx, *, rope, emit_kv32, tm):
    B, T, _ = x.shape
    nt = T // tm
    mod_idx = (lambda b, i: (b + mod_row, 0, 0)) if mod_row else (lambda b, i: (0, 0, 0))
    in_specs = [
        pl.BlockSpec((1, tm, D_MODEL), lambda b, i: (b, i, 0)),
        pl.BlockSpec((1, 1, 6 * D_MODEL), mod_idx),
        _resident((D_MODEL, PROJ_PAD)),
        _resident((HEAD_DIM, tm)),
        _resident((HEAD_DIM, tm)),
    ]
    args = [x, mod, w_in, gq, gk]
    if rope:
        in_specs += [pl.BlockSpec((HEAD_DIM, tm), lambda b, i: (0, i))] * 2
        args += [cos, sin]
    in_specs += [
        _resident((1, GATE_PAD)),
        _resident((1, G_WIDTH)),
        _resident((1, G_WIDTH)),
        _resident((G_GROUPS, CHUNK, CHUNK)),
        _resident((CHUNK, G_WIDTH)),
    ]
    args += [gate_b, lng, lnb, ws, bsx]
    out_shape = [
        jax.ShapeDtypeStruct((B, ATT_HEADS, HEAD_DIM, T), BF16),
        jax.ShapeDtypeStruct((B, ATT_KV_HEADS, T, HEAD_DIM), BF16),
        jax.ShapeDtypeStruct((B, ATT_KV_HEADS, V_ROWS, T), BF16),
        jax.ShapeDtypeStruct((B, M_HEADS, T, M_DIM), BF16),
        jax.ShapeDtypeStruct((B, M_HEADS, M_DIM, T), BF16),
        jax.ShapeDtypeStruct((B, M_HEADS, T, M_DIM), BF16),
        jax.ShapeDtypeStruct((B, T, M_WIDTH), BF16),
        jax.ShapeDtypeStruct((B, T, G_WIDTH), BF16),
        jax.ShapeDtypeStruct((B, T, GATE_PAD), F32),
        jax.ShapeDtypeStruct((B, N_GATES, T), F32),
    ]
    out_specs = [
        pl.BlockSpec((1, ATT_HEADS, HEAD_DIM, tm), lambda b, i: (b, 0, 0, i)),
        pl.BlockSpec((1, ATT_KV_HEADS, tm, HEAD_DIM), lambda b, i: (b, 0, i, 0)),
        pl.BlockSpec((1, ATT_KV_HEADS, V_ROWS, tm), lambda b, i: (b, 0, 0, i)),
        pl.BlockSpec((1, M_HEADS, tm, M_DIM), lambda b, i: (b, 0, i, 0)),
        pl.BlockSpec((1, M_HEADS, M_DIM, tm), lambda b, i: (b, 0, 0, i)),
        pl.BlockSpec((1, M_HEADS, tm, M_DIM), lambda b, i: (b, 0, i, 0)),
        pl.BlockSpec((1, tm, M_WIDTH), lambda b, i: (b, i, 0)),
        pl.BlockSpec((1, tm, G_WIDTH), lambda b, i: (b, i, 0)),
        pl.BlockSpec((1, tm, GATE_PAD), lambda b, i: (b, i, 0)),
        pl.BlockSpec((1, N_GATES, tm), lambda b, i: (b, 0, i)),
    ]
    if emit_kv32:
        out_shape += [jax.ShapeDtypeStruct((B, T, KV_WIDTH), F32)] * 2
        out_specs += [pl.BlockSpec((1, tm, KV_WIDTH), lambda b, i: (b, i, 0))] * 2
    return pl.pallas_call(
        functools.partial(_inproj_kernel, rope=rope, emit_kv32=emit_kv32, tm=tm),
        grid=(B, nt),
        in_specs=in_specs,
        out_specs=out_specs,
        out_shape=out_shape,
        compiler_params=pltpu.CompilerParams(
            dimension_semantics=("parallel", "parallel"), vmem_limit_bytes=VMEM_LIMIT),
        name="inproj",
    )(*args)


def _attn_kernel(qT_ref, k_ref, v1T_ref, o_ref, acc_ref, m_ref, *, n_keys, kc, tq):
    nk = n_keys // kc
    outs = []
    for g in range(ATT_GROUP):
        qT = qT_ref[0, g]
        m_ref[...] = jnp.full(m_ref.shape, -jnp.inf, F32)
        acc_ref[...] = jnp.zeros(acc_ref.shape, F32)

        def body(j, carry, qT=qT):
            ks = pl.multiple_of(j * kc, kc)
            sT = _dot(k_ref[0, 0, pl.ds(ks, kc), :], qT)
            m_old = m_ref[...]
            m_new = jnp.maximum(m_old, jnp.max(sT, axis=0, keepdims=True))
            alpha = jnp.exp(m_old - m_new)
            p = jnp.exp(sT - m_new).astype(BF16)
            pv = _dot(v1T_ref[0, 0, :, pl.ds(ks, kc)], p)
            acc_ref[...] = alpha * acc_ref[...] + pv
            m_ref[...] = m_new
            return carry

        lax.fori_loop(0, nk, body, 0)
        acc = acc_ref[...]
        outs.append(acc[0:HEAD_DIM] / acc[HEAD_DIM:HEAD_DIM + 1])
    o_ref[0] = jnp.concatenate(outs, axis=0).T.astype(BF16)


def _attention(qT, k, v1T, *, tq, kc):
    B, _, _, T = qT.shape
    n_keys = k.shape[2]
    return pl.pallas_call(
        functools.partial(_attn_kernel, n_keys=n_keys, kc=kc, tq=tq),
        grid=(B, ATT_KV_HEADS, T // tq),
        in_specs=[
            pl.BlockSpec((1, ATT_GROUP, HEAD_DIM, tq), lambda b, h, i: (b, h, 0, i)),
            pl.BlockSpec((1, 1, n_keys, HEAD_DIM), lambda b, h, i: (b, h, 0, 0)),
            pl.BlockSpec((1, 1, V_ROWS, n_keys), lambda b, h, i: (b, h, 0, 0)),
        ],
        out_specs=pl.BlockSpec((1, tq, ATT_GROUP * HEAD_DIM), lambda b, h, i: (b, i, h)),
        out_shape=jax.ShapeDtypeStruct((B, T, ATT_WIDTH), BF16),
        scratch_shapes=[pltpu.VMEM((V_ROWS, tq), F32), pltpu.VMEM((1, tq), F32)],
        compiler_params=pltpu.CompilerParams(
            dimension_semantics=("parallel", "parallel", "parallel"), vmem_limit_bytes=VMEM_LIMIT),
        name="attention",
    )(qT, k, v1T)


def _mlstm_kernel(qf_ref, kTf_ref, vf_ref, qb_ref, kTb_ref, vb_ref, gcf_ref, gcb_ref, grf_ref, grb_ref,
                  s0_ref, m0_ref, hf_ref, hb_ref, sout_ref, mout_ref, s_scr, m_scr, *, nc):
    c = pl.program_id(1)

    @pl.when(c == 0)
    def _():
        s_scr[...] = s0_ref[0]
        m_scr[...] = m0_ref[0]

    ti = lax.broadcasted_iota(jnp.int32, (CHUNK, CHUNK), 0)
    si = lax.broadcasted_iota(jnp.int32, (CHUNK, CHUNK), 1)
    lower = si <= ti
    upper = si >= ti
    lower_b = lower.astype(BF16)
    upper_b = upper.astype(BF16)
    ones_v = jnp.ones((CHUNK, M_DIM), BF16)

    for d in range(2):
        q_ref, kT_ref, v_ref = (qf_ref, kTf_ref, vf_ref) if d == 0 else (qb_ref, kTb_ref, vb_ref)
        gc = (gcf_ref if d == 0 else gcb_ref)[0]
        gr = (grf_ref if d == 0 else grb_ref)[0]
        valid = lower if d == 0 else upper
        tri_col = lower_b if d == 0 else upper_b
        tri_row = upper_b if d == 0 else lower_b
        gc_hi = gc.astype(BF16)
        gc_lo = (gc - gc_hi.astype(F32)).astype(BF16)
        acol_all = _dot(tri_col, gc_hi) + _dot(tri_col, gc_lo)
        gr_hi = gr.astype(BF16)
        gr_lo = (gr - gr_hi.astype(F32)).astype(BF16)
        arow_all = _dot(gr_hi, tri_row) + _dot(gr_lo, tri_row)
        last = CHUNK - 1 if d == 0 else 0
        outs = []
        for hd in range(M_HEADS):
            j = M_HEADS * d + hd
            ci = 2 * M_HEADS * d + hd
            a_col = acol_all[:, ci + M_HEADS:ci + M_HEADS + 1]
            a_row = arow_all[ci + M_HEADS:ci + M_HEADS + 1, :]
            r_row = gr[ci:ci + 1, :] - a_row
            rb = jnp.where(valid, r_row, -jnp.inf)
            R_col = jnp.max(rb, axis=1, keepdims=True)
            dmat = jnp.exp(rb - R_col)
            qh = q_ref[0, hd]
            khT = kT_ref[0, hd]
            v1 = jnp.concatenate([v_ref[0, hd], ones_v], axis=1)
            s0 = (_dot(qh, khT) * dmat).astype(BF16)
            p = _dot(s0, v1)
            R_L = jnp.max(r_row, axis=1, keepdims=True)
            kw = (khT.astype(F32) * jnp.exp(r_row - R_L)).astype(BF16)
            u = _dot(kw, v1)
            m = m_scr[j][:, 0:1]
            st = s_scr[j]
            M_col = jnp.maximum(m, R_col)
            qc = _dot(qh, st.astype(BF16))
            tot = jnp.exp(R_col - M_col) * p + jnp.exp(m - M_col) * qc
            den = jnp.maximum(jnp.abs(tot), jnp.exp(-(a_col + M_col)))
            outs.append(tot[:, 0:M_DIM] / den[:, M_DIM:2 * M_DIM])
            M_L = jnp.maximum(m, R_L)
            s_scr[j] = jnp.exp(m - M_L) * st + jnp.exp(R_L - M_L) * u
            m_scr[j] = jnp.broadcast_to(a_row[:, last:last + 1] + M_L, (1, LANES))
        (hf_ref if d == 0 else hb_ref)[0] = jnp.concatenate(outs, axis=1)

    @pl.when(c == nc - 1)
    def _():
        sout_ref[0] = s_scr[...]
        mout_ref[0] = m_scr[...]


def _mlstm(qm, kmT, vm, gcol, grow, s0, m0):
    B, _, T, _ = qm.shape
    nc = T // CHUNK
    fwd3 = lambda b, c: (b, 0, c, 0)
    bwd3 = lambda b, c: (b, 0, nc - 1 - c, 0)
    fwdT = lambda b, c: (b, 0, 0, c)
    bwdT = lambda b, c: (b, 0, 0, nc - 1 - c)
    qspec = lambda im: pl.BlockSpec((1, M_HEADS, CHUNK, M_DIM), im)
    kspec = lambda im: pl.BlockSpec((1, M_HEADS, M_DIM, CHUNK), im)
    n_state = 2 * M_HEADS
    return pl.pallas_call(
        functools.partial(_mlstm_kernel, nc=nc),
        grid=(B, nc),
        in_specs=[
            qspec(fwd3), kspec(fwdT), qspec(fwd3),
            qspec(bwd3), kspec(bwdT), qspec(bwd3),
            pl.BlockSpec((1, CHUNK, GATE_PAD), lambda b, c: (b, c, 0)),
            pl.BlockSpec((1, CHUNK, GATE_PAD), lambda b, c: (b, nc - 1 - c, 0)),
            pl.BlockSpec((1, N_GATES, CHUNK), lambda b, c: (b, 0, c)),
            pl.BlockSpec((1, N_GATES, CHUNK), lambda b, c: (b, 0, nc - 1 - c)),
            pl.BlockSpec((1, n_state, M_DIM, LANES), lambda b, c: (b, 0, 0, 0)),
            pl.BlockSpec((1, n_state, 1, LANES), lambda b, c: (b, 0, 0, 0)),
        ],
        out_specs=[
            pl.BlockSpec((1, CHUNK, M_WIDTH), lambda b, c: (b, c, 0)),
            pl.BlockSpec((1, CHUNK, M_WIDTH), lambda b, c: (b, nc - 1 - c, 0)),
            pl.BlockSpec((1, n_state, M_DIM, LANES), lambda b, c: (b, 0, 0, 0)),
            pl.BlockSpec((1, n_state, 1, LANES), lambda b, c: (b, 0, 0, 0)),
        ],
        out_shape=[
            jax.ShapeDtypeStruct((B, T, M_WIDTH), F32),
            jax.ShapeDtypeStruct((B, T, M_WIDTH), F32),
            jax.ShapeDtypeStruct((B, n_state, M_DIM, LANES), F32),
            jax.ShapeDtypeStruct((B, n_state, 1, LANES), F32),
        ],
        scratch_shapes=[pltpu.VMEM((n_state, M_DIM, LANES), F32), pltpu.VMEM((n_state, 1, LANES), F32)],
        compiler_params=pltpu.CompilerParams(
            dimension_semantics=("parallel", "arbitrary"), vmem_limit_bytes=VMEM_LIMIT),
        name="mlstm",
    )(qm, kmT, vm, qm, kmT, vm, gcol, gcol, grow, grow, s0, m0)


def _post_kernel(att_ref, hf_ref, hb_ref, so_ref, go_ref, x_ref, mod_ref, wo_ref, l1g_ref, l1b_ref,
                 w1_ref, w2_ref, l2g_ref, l2b_ref, o_ref, *, ff_chunk):
    mod = mod_ref[0]
    g1 = mod[:, 2 * D_MODEL:3 * D_MODEL]
    sh2 = mod[:, 3 * D_MODEL:4 * D_MODEL]
    sc2 = mod[:, 4 * D_MODEL:5 * D_MODEL]
    g2 = mod[:, 5 * D_MODEL:6 * D_MODEL]
    m_out = (so_ref[0].astype(F32) * (hf_ref[0] + hb_ref[0])).astype(BF16)
    cat = jnp.concatenate([att_ref[0], m_out, go_ref[0]], axis=1)
    y = _dot(cat, wo_ref[...])
    x1 = _layer_norm(DEEPNORM_ALPHA * x_ref[0] + g1 * y, l1g_ref[...], l1b_ref[...])
    h2 = (x1 * (1.0 + sc2) + sh2).astype(BF16)
    f = None
    for j in range(D_FF // ff_chunk):
        a = jnp.maximum(_dot(h2, w1_ref[:, j * ff_chunk:(j + 1) * ff_chunk]), 0.0)
        part = _dot((a * a).astype(BF16), w2_ref[j * ff_chunk:(j + 1) * ff_chunk, :])
        f = part if f is None else f + part
    o_ref[0] = _layer_norm(DEEPNORM_ALPHA * x1 + g2 * f, l2g_ref[...], l2b_ref[...])


def _post(att, hf, hb, so, go, x, mod, mod_row, wo, l1g, l1b, w1, w2, l2g, l2b, *, tm, ff_chunk):
    B, T, _ = x.shape
    mod_idx = (lambda b, i: (b + mod_row, 0, 0)) if mod_row else (lambda b, i: (0, 0, 0))
    tok = lambda w: pl.BlockSpec((1, tm, w), lambda b, i: (b, i, 0))
    return pl.pallas_call(
        functools.partial(_post_kernel, ff_chunk=ff_chunk),
        grid=(B, T // tm),
        in_specs=[
            tok(ATT_WIDTH), tok(M_WIDTH), tok(M_WIDTH), tok(M_WIDTH), tok(G_WIDTH), tok(D_MODEL),
            pl.BlockSpec((1, 1, 6 * D_MODEL), mod_idx),
            _resident((D_MODEL, D_MODEL)), _resident((1, D_MODEL)), _resident((1, D_MODEL)),
            _resident((D_MODEL, D_FF)), _resident((D_FF, D_MODEL)),
            _resident((1, D_MODEL)), _resident((1, D_MODEL)),
        ],
        out_specs=tok(D_MODEL),
        out_shape=jax.ShapeDtypeStruct((B, T, D_MODEL), F32),
        compiler_params=pltpu.CompilerParams(
            dimension_semantics=("parallel", "parallel"), vmem_limit_bytes=VMEM_LIMIT),
        name="post",
    )(att, hf, hb, so, go, x, mod, wo, l1g, l1b, w1, w2, l2g, l2b)


def _rope_tables(T):
    rows = T // GRID_W
    r, cidx = jnp.meshgrid(jnp.arange(rows, dtype=F32), jnp.arange(GRID_W, dtype=F32), indexing='ij')
    n_freq = HEAD_DIM // 4
    inv = ROPE_BASE ** (-jnp.arange(n_freq, dtype=F32) / n_freq)
    ang_r = (r.reshape(-1)[:, None] * inv[None, :]).T
    ang_c = (cidx.reshape(-1)[:, None] * inv[None, :]).T
    cos = jnp.concatenate([jnp.cos(ang_r)] * 2 + [jnp.cos(ang_c)] * 2, axis=0)
    sin = jnp.concatenate([-jnp.sin(ang_r), jnp.sin(ang_r), -jnp.sin(ang_c), jnp.sin(ang_c)], axis=0)
    return cos, sin


def _with_ones_rows(vT):
    ones = jnp.ones(vT.shape[:-2] + (BF16_SUBLANES, vT.shape[-1]), vT.dtype)
    return jnp.concatenate([vT, ones], axis=-2)


def _pack_state(C, n, m):
    B = C.shape[0]
    ns = 2 * M_HEADS
    C = C.reshape(B, ns, M_DIM, M_DIM).astype(F32)
    n = jnp.broadcast_to(n.reshape(B, ns, M_DIM, 1).astype(F32), (B, ns, M_DIM, LANES - M_DIM))
    m = jnp.broadcast_to(m.reshape(B, ns, 1, 1).astype(F32), (B, ns, 1, LANES))
    return jnp.concatenate([C, n], axis=-1), m


def kernel(x_prompt, x_sample, c, cache_attn_k, cache_attn_v, state_mlstm_C, state_mlstm_n, state_mlstm_m,
           c_ctx, w_mod, b_mod, w_in, q_norm_g, k_norm_g, mlstm_gate_b, gmlp_ln_g, gmlp_ln_b, gmlp_ws,
           gmlp_bs, w_out, ln1_g, ln1_b, w_ff1, w_ff2, ln2_g, ln2_b):
    BP, TP, _ = x_prompt.shape
    BS, TS, _ = x_sample.shape
    n_mod = 16
    cvec = jnp.zeros((n_mod, D_MODEL), F32).at[0].set(c_ctx).at[1:1 + BS].set(c)
    mod_all = _modulation(cvec, w_mod, b_mod).reshape(DEPTH, n_mod, 1, 6 * D_MODEL)

    cos_s, sin_s = _rope_tables(TS)
    tm_p, tm_s = TP, 512

    xp, xs = x_prompt, x_sample
    ks_, vs_, Cs_, ns_, ms_ = [], [], [], [], []
    for l in range(DEPTH):
        wl = w_in[l]
        g0 = OFF_OM + M_WIDTH
        w_in_l = jnp.concatenate(
            [wl[:, :g0], wl[:, g0 + N_GATES:], wl[:, g0:g0 + N_GATES],
             jnp.zeros((D_MODEL, GATE_PAD - N_GATES), F32)], axis=1).astype(BF16)
        gate_b = jnp.zeros((1, GATE_PAD), F32).at[0, :N_GATES].set(mlstm_gate_b[l].reshape(-1))
        lng = gmlp_ln_g[l].reshape(1, G_WIDTH)
        lnb = gmlp_ln_b[l].reshape(1, G_WIDTH)
        ws = gmlp_ws[l].astype(BF16)
        bsx = jnp.repeat(gmlp_bs[l].T, G_CH, axis=1)
        wo = w_out[l].astype(BF16)
        w1 = w_ff1[l].astype(BF16)
        w2 = w_ff2[l].astype(BF16)
        l1g, l1b = ln1_g[l].reshape(1, -1), ln1_b[l].reshape(1, -1)
        l2g, l2b = ln2_g[l].reshape(1, -1), ln2_b[l].reshape(1, -1)
        mod = mod_all[l]

        def gains(tm):
            gq = jnp.broadcast_to((q_norm_g[l] * (HEAD_DIM ** -0.5))[:, None], (HEAD_DIM, tm))
            gk = jnp.broadcast_to(k_norm_g[l][:, None], (HEAD_DIM, tm))
            return gq, gk

        gq, gk = gains(tm_p)
        (qT, k, v1T, qm, kmT, vm, so, go, gcol, grow, k32, v32) = _inproj(
            xp, mod, 0, w_in_l, gq, gk, None, None, gate_b, lng, lnb, ws, bsx,
            rope=False, emit_kv32=True, tm=tm_p)
        att = _attention(qT, k, v1T, tq=TP, kc=TP)
        zero_state = _pack_state(jnp.zeros((BP, 2, M_HEADS, M_DIM, M_DIM), F32),
                                 jnp.zeros((BP, 2, M_HEADS, M_DIM), F32),
                                 jnp.zeros((BP, 2, M_HEADS), F32))
        hf, hb, s_fin, m_fin = _mlstm(qm, kmT, vm, gcol, grow, *zero_state)
        xp = _post(att, hf, hb, so, go, xp, mod, 0, wo, l1g, l1b, w1, w2, l2g, l2b, tm=tm_p, ff_chunk=1024)
        ks_.append(k32.reshape(BP, TP, ATT_KV_HEADS, HEAD_DIM))
        vs_.append(v32.reshape(BP, TP, ATT_KV_HEADS, HEAD_DIM))
        Cs_.append(s_fin[..., :M_DIM].reshape(BP, 2, M_HEADS, M_DIM, M_DIM))
        ns_.append(s_fin[..., M_DIM].reshape(BP, 2, M_HEADS, M_DIM))
        ms_.append(m_fin[:, :, 0, 0].reshape(BP, 2, M_HEADS))

        gq, gk = gains(tm_s)
        (qT, k, v1T, qm, kmT, vm, so, go, gcol, grow) = _inproj(
            xs, mod, 1, w_in_l, gq, gk, cos_s, sin_s, gate_b, lng, lnb, ws, bsx,
            rope=True, emit_kv32=False, tm=tm_s)
        k_c = jnp.transpose(cache_attn_k[:, l], (0, 2, 1, 3)).astype(BF16)
        vT_c = _with_ones_rows(jnp.transpose(cache_attn_v[:, l], (0, 2, 3, 1)).astype(BF16))
        k_all = jnp.concatenate([k_c, k], axis=2)
        v1T_all = jnp.concatenate([vT_c, v1T], axis=3)
        att = _attention(qT, k_all, v1T_all, tq=512, kc=512)
        state = _pack_state(state_mlstm_C[:, l], state_mlstm_n[:, l], state_mlstm_m[:, l])
        hf, hb, _, _ = _mlstm(qm, kmT, vm, gcol, grow, *state)
        xs = _post(att, hf, hb, so, go, xs, mod, 1, wo, l1g, l1b, w1, w2, l2g, l2b, tm=tm_s, ff_chunk=1024)

    return (xp, xs, jnp.stack(ks_, axis=1), jnp.stack(vs_, axis=1), jnp.stack(Cs_, axis=1),
            jnp.stack(ns_, axis=1), jnp.stack(ms_, axis=1))
```

```python
import functools

import jax
import jax.numpy as jnp
from jax import lax
from jax.experimental import pallas as pl
from jax.experimental.pallas import tpu as pltpu

D_MODEL = 1024
DEPTH = 2
GRID_W = 64
HEAD_DIM = 64
ATT_HEADS = 8
ATT_KV_HEADS = 2
ATT_GROUP = ATT_HEADS // ATT_KV_HEADS
ATT_WIDTH = ATT_HEADS * HEAD_DIM
KV_WIDTH = ATT_KV_HEADS * HEAD_DIM
M_HEADS = 4
M_DIM = 64
M_WIDTH = M_HEADS * M_DIM
G_GROUPS = 4
G_WIDTH = D_MODEL - ATT_WIDTH - M_WIDTH
G_CH = G_WIDTH // G_GROUPS
CHUNK = 128
D_FF = 4 * D_MODEL
ROPE_BASE = 10000.0
EPS = 1e-6
DEEPNORM_ALPHA = (2 * DEPTH) ** 0.25
N_GATES = 4 * M_HEADS

LANES = 128
BF16_SUBLANES = 16
GATE_PAD = LANES
OFF_QA = 0
OFF_KA = OFF_QA + ATT_WIDTH
OFF_VA = OFF_KA + KV_WIDTH
OFF_QM = OFF_VA + KV_WIDTH
OFF_KM = OFF_QM + M_WIDTH
OFF_VM = OFF_KM + M_WIDTH
OFF_OM = OFF_VM + M_WIDTH
OFF_UG = OFF_OM + M_WIDTH
OFF_VG = OFF_UG + G_WIDTH
OFF_GM = OFF_VG + G_WIDTH
PROJ_PAD = OFF_GM + GATE_PAD
V_ROWS = HEAD_DIM + BF16_SUBLANES
VMEM_LIMIT = 56 * 1024 * 1024

ATT_TQ, ATT_KC, ATT_HPB = 512, 512, 2
LOG2_E = 1.4426950408889634
BOUND_MARGIN = 1.0 + 2.0 ** -6
MAX_FAST_SHIFT = 60.0

F32 = jnp.float32
BF16 = jnp.bfloat16


def _dot(a, b):
    return jnp.dot(a, b, preferred_element_type=F32)


def _resident(shape):
    nd = len(shape)
    return pl.BlockSpec(shape, lambda *_: (0,) * nd, pipeline_mode=pl.Buffered(1))


def _sigmoid(x):
    return 1.0 / (1.0 + jnp.exp(-x))


def _log_sigmoid(x):
    return jnp.minimum(x, 0.0) - jnp.log1p(jnp.exp(-jnp.abs(x)))


def _layer_norm(x, g, b):
    mu = jnp.mean(x, axis=-1, keepdims=True)
    xc = x - mu
    var = jnp.mean(xc * xc, axis=-1, keepdims=True)
    return xc * lax.rsqrt(var + EPS) * g + b


def _mod_kernel(c_ref, w_ref, b_ref, o_ref):
    c = c_ref[...]
    s = (c * _sigmoid(c)).astype(BF16)
    o_ref[0] = _dot(s, w_ref[0].astype(BF16)) + b_ref[0]


def _modulation(cvec, w_mod, b_mod):
    n = cvec.shape[0]
    bn = 1536
    return pl.pallas_call(
        _mod_kernel,
        grid=(DEPTH, 6 * D_MODEL // bn),
        in_specs=[
            pl.BlockSpec((n, D_MODEL), lambda l, j: (0, 0)),
            pl.BlockSpec((1, D_MODEL, bn), lambda l, j: (l, 0, j)),
            pl.BlockSpec((1, 1, bn), lambda l, j: (l, 0, j)),
        ],
        out_specs=pl.BlockSpec((1, n, bn), lambda l, j: (l, 0, j)),
        out_shape=jax.ShapeDtypeStruct((DEPTH, n, 6 * D_MODEL), F32),
        compiler_params=pltpu.CompilerParams(
            dimension_semantics=("arbitrary", "arbitrary"), vmem_limit_bytes=VMEM_LIMIT),
        name="modulation",
    )(cvec, w_mod, b_mod.reshape(DEPTH, 1, 6 * D_MODEL))


def _norm_rope_heads(zT, gain, cos, sin, n_heads):
    outs = []
    for hd in range(n_heads):
        t = zT[hd * HEAD_DIM:(hd + 1) * HEAD_DIM, :]
        ms = jnp.mean(t * t, axis=0, keepdims=True)
        y = t * lax.rsqrt(ms + EPS) * gain
        if cos is not None:
            q = HEAD_DIM // 4
            sw = jnp.concatenate([y[q:2 * q], y[0:q], y[3 * q:4 * q], y[2 * q:3 * q]], axis=0)
            y = y * cos + sw * sin
        outs.append(y)
    return outs


def _inproj_kernel(*refs, rope, emit_kv32, tm):
    it = iter(refs)
    x_ref, mod_ref, w_ref, gq_ref, gk_ref = (next(it) for _ in range(5))
    cos_ref, sin_ref = (next(it), next(it)) if rope else (None, None)
    gb_ref, lng_ref, lnb_ref, ws_ref, bsx_ref = (next(it) for _ in range(5))
    qT_ref, k_ref, vT_ref, qm_ref, kmT_ref, vm_ref, so_ref, go_ref, gcol_ref, grow_ref = (
        next(it) for _ in range(10))
    k32_ref, v32_ref = (next(it), next(it)) if emit_kv32 else (None, None)

    x = x_ref[0]
    mod = mod_ref[0]
    sh1 = mod[:, 0:D_MODEL]
    sc1 = mod[:, D_MODEL:2 * D_MODEL]
    h = (x * (1.0 + sc1) + sh1).astype(BF16)

    cos = cos_ref[...] if rope else None
    sin = sin_ref[...] if rope else None

    zq = _dot(h, w_ref[:, OFF_QA:OFF_QA + ATT_WIDTH])
    q_heads = _norm_rope_heads(zq.T, gq_ref[...], cos, sin, ATT_HEADS)
    for hd in range(ATT_HEADS):
        qT_ref[0, hd] = q_heads[hd].astype(BF16)

    zk = _dot(h, w_ref[:, OFF_KA:OFF_KA + KV_WIDTH])
    k_heads = _norm_rope_heads(zk.T, gk_ref[...], cos, sin, ATT_KV_HEADS)
    kk = jnp.concatenate(k_heads, axis=0).T
    ones_col = (lax.broadcasted_iota(jnp.int32, (tm, LANES - HEAD_DIM), 1) == 0).astype(F32)
    for hd in range(ATT_KV_HEADS):
        k_ref[0, hd] = jnp.concatenate(
            [kk[:, hd * HEAD_DIM:(hd + 1) * HEAD_DIM], ones_col], axis=1).astype(BF16)
    zv = _dot(h, w_ref[:, OFF_VA:OFF_VA + KV_WIDTH])
    vT = zv.T
    ones_rows = jnp.ones((BF16_SUBLANES, tm), BF16)
    for hd in range(ATT_KV_HEADS):
        vT_ref[0, hd] = jnp.concatenate(
            [vT[hd * HEAD_DIM:(hd + 1) * HEAD_DIM].astype(BF16), ones_rows], axis=0)
    if emit_kv32:
        k32_ref[0] = kk
        v32_ref[0] = zv

    zqm = _dot(h, w_ref[:, OFF_QM:OFF_QM + M_WIDTH])
    zkmT = (_dot(h, w_ref[:, OFF_KM:OFF_KM + M_WIDTH]) * (M_DIM ** -0.5)).T
    zvm = _dot(h, w_ref[:, OFF_VM:OFF_VM + M_WIDTH])
    for hd in range(M_HEADS):
        qm_ref[0, hd] = zqm[:, hd * M_DIM:(hd + 1) * M_DIM].astype(BF16)
        kmT_ref[0, hd] = zkmT[hd * M_DIM:(hd + 1) * M_DIM, :].astype(BF16)
        vm_ref[0, hd] = zvm[:, hd * M_DIM:(hd + 1) * M_DIM].astype(BF16)
    zo = _dot(h, w_ref[:, OFF_OM:OFF_OM + M_WIDTH])
    so_ref[0] = _sigmoid(zo).astype(BF16)

    zg = _dot(h, w_ref[:, OFF_GM:OFF_GM + GATE_PAD]) + gb_ref[...]
    lane = lax.broadcasted_iota(jnp.int32, zg.shape, 1)
    is_forget = (lane % (2 * M_HEADS)) >= M_HEADS
    gp = jnp.where(is_forget, _log_sigmoid(zg), zg)
    gcol_ref[0] = gp
    grow_ref[0] = gp.T[0:N_GATES, :]

    zu = _dot(h, w_ref[:, OFF_UG:OFF_UG + G_WIDTH])
    zvg = _dot(h, w_ref[:, OFF_VG:OFF_VG + G_WIDTH])
    vn = _layer_norm(zvg, lng_ref[...], lnb_ref[...])
    group = lax.broadcasted_iota(jnp.int32, (CHUNK, G_WIDTH), 1) // G_CH
    bsx = bsx_ref[...]
    s_chunks = []
    for j in range(tm // CHUNK):
        vc = vn[j * CHUNK:(j + 1) * CHUNK, :]
        acc = bsx
        for g in range(G_GROUPS):
            acc = acc + _dot(ws_ref[g], jnp.where(group == g, vc, 0.0).astype(BF16))
        s_chunks.append(acc)
    s = jnp.concatenate(s_chunks, axis=0) if len(s_chunks) > 1 else s_chunks[0]
    go_ref[0] = (zu * s).astype(BF16)


def _inproj(x, mod, mod_row, w_in, gq, gk, cos, sin, gate_b, lng, lnb, ws, bsx, *, rope, emit_kv32, tm):
    B, T, _ = x.shape
    nt = T // tm
    mod_idx = (lambda b, i: (b + mod_row, 0, 0)) if mod_row else (lambda b, i: (0, 0, 0))
    in_specs = [
        pl.BlockSpec((1, tm, D_MODEL), lambda b, i: (b, i, 0)),
        pl.BlockSpec((1, 1, 6 * D_MODEL), mod_idx),
        _resident((D_MODEL, PROJ_PAD)),
        _resident((HEAD_DIM, tm)),
        _resident((HEAD_DIM, tm)),
    ]
    args = [x, mod, w_in, gq, gk]
    if rope:
        in_specs += [pl.BlockSpec((HEAD_DIM, tm), lambda b, i: (0, i))] * 2
        args += [cos, sin]
    in_specs += [
        _resident((1, GATE_PAD)),
        _resident((1, G_WIDTH)),
        _resident((1, G_WIDTH)),
        _resident((G_GROUPS, CHUNK, CHUNK)),
        _resident((CHUNK, G_WIDTH)),
    ]
    args += [gate_b, lng, lnb, ws, bsx]
    out_shape = [
        jax.ShapeDtypeStruct((B, ATT_HEADS, HEAD_DIM, T), BF16),
        jax.ShapeDtypeStruct((B, ATT_KV_HEADS, T, LANES), BF16),
        jax.ShapeDtypeStruct((B, ATT_KV_HEADS, V_ROWS, T), BF16),
        jax.ShapeDtypeStruct((B, M_HEADS, T, M_DIM), BF16),
        jax.ShapeDtypeStruct((B, M_HEADS, M_DIM, T), BF16),
        jax.ShapeDtypeStruct((B, M_HEADS, T, M_DIM), BF16),
        jax.ShapeDtypeStruct((B, T, M_WIDTH), BF16),
        jax.ShapeDtypeStruct((B, T, G_WIDTH), BF16),
        jax.ShapeDtypeStruct((B, T, GATE_PAD), F32),
        jax.ShapeDtypeStruct((B, N_GATES, T), F32),
    ]
    out_specs = [
        pl.BlockSpec((1, ATT_HEADS, HEAD_DIM, tm), lambda b, i: (b, 0, 0, i)),
        pl.BlockSpec((1, ATT_KV_HEADS, tm, LANES), lambda b, i: (b, 0, i, 0)),
        pl.BlockSpec((1, ATT_KV_HEADS, V_ROWS, tm), lambda b, i: (b, 0, 0, i)),
        pl.BlockSpec((1, M_HEADS, tm, M_DIM), lambda b, i: (b, 0, i, 0)),
        pl.BlockSpec((1, M_HEADS, M_DIM, tm), lambda b, i: (b, 0, 0, i)),
        pl.BlockSpec((1, M_HEADS, tm, M_DIM), lambda b, i: (b, 0, i, 0)),
        pl.BlockSpec((1, tm, M_WIDTH), lambda b, i: (b, i, 0)),
        pl.BlockSpec((1, tm, G_WIDTH), lambda b, i: (b, i, 0)),
        pl.BlockSpec((1, tm, GATE_PAD), lambda b, i: (b, i, 0)),
        pl.BlockSpec((1, N_GATES, tm), lambda b, i: (b, 0, i)),
    ]
    if emit_kv32:
        out_shape += [jax.ShapeDtypeStruct((B, T, KV_WIDTH), F32)] * 2
        out_specs += [pl.BlockSpec((1, tm, KV_WIDTH), lambda b, i: (b, i, 0))] * 2
    return pl.pallas_call(
        functools.partial(_inproj_kernel, rope=rope, emit_kv32=emit_kv32, tm=tm),
        grid=(B, nt),
        in_specs=in_specs,
        out_specs=out_specs,
        out_shape=out_shape,
        compiler_params=pltpu.CompilerParams(
            dimension_semantics=("parallel", "parallel"), vmem_limit_bytes=VMEM_LIMIT),
        name="inproj",
    )(*args)


def _attn_kernel(qT_ref, k1_ref, v1T_ref, o_ref, oT_scr, q1_scr, kmax_scr, acc_scr, m_scr,
                 *, n_keys, kc, tq, hpb):
    nk = n_keys // kc
    i = pl.program_id(2)

    @pl.when(i == 0)
    def _():
        kf = k1_ref[0, 0].astype(F32)
        n2 = jnp.sum(kf * kf, axis=1, keepdims=True) - 1.0
        kmax_scr[...] = jnp.broadcast_to(jnp.max(n2, axis=0, keepdims=True), kmax_scr.shape)

    kmax2 = kmax_scr[0:1, 0:1]
    row0 = lax.broadcasted_iota(jnp.int32, (BF16_SUBLANES, tq), 0) == 0
    pad = jnp.zeros((LANES - HEAD_DIM - BF16_SUBLANES, tq), BF16)
    cmax = None
    for g in range(ATT_GROUP):
        qT = qT_ref[0, g]
        qf = qT.astype(F32)
        c = jnp.sqrt(jnp.sum(qf * qf, axis=0, keepdims=True) * kmax2) * BOUND_MARGIN
        shift = jnp.where(row0, -c, 0.0).astype(BF16)
        q1_scr[g] = jnp.concatenate([qT, shift, pad], axis=0)
        cg = jnp.max(c)
        cmax = cg if cmax is None else jnp.maximum(cmax, cg)
    fast = cmax <= MAX_FAST_SHIFT

    @pl.when(fast)
    def _():
        def heads(gi, carry):
            gs = [gi * hpb + a for a in range(hpb)]
            q1s = [q1_scr[g] for g in gs]

            def scores(j):
                kblk = k1_ref[0, 0, j * kc:(j + 1) * kc, :]
                return [_dot(kblk, q1s[a]) for a in range(hpb)]

            accs = [None] * hpb
            s_next = scores(0)
            for j in range(nk):
                s_cur = s_next
                if j + 1 < nk:
                    s_next = scores(j + 1)
                vblk = v1T_ref[0, 0, :, j * kc:(j + 1) * kc]
                for a in range(hpb):
                    pv = _dot(vblk, jnp.exp2(s_cur[a]).astype(BF16))
                    accs[a] = pv if j == 0 else accs[a] + pv
            for a in range(hpb):
                oT_scr[gs[a]] = accs[a][0:HEAD_DIM] / accs[a][HEAD_DIM:HEAD_DIM + 1]
            return carry

        lax.fori_loop(0, ATT_GROUP // hpb, heads, 0)

    @pl.when(jnp.logical_not(fast))
    def _():
        def head(g, carry):
            qT = qT_ref[0, g]
            m_scr[...] = jnp.full(m_scr.shape, -jnp.inf, F32)
            acc_scr[...] = jnp.zeros(acc_scr.shape, F32)

            def chunk(j, c2):
                ks = pl.multiple_of(j * kc, kc)
                sT = _dot(k1_ref[0, 0, pl.ds(ks, kc), 0:HEAD_DIM], qT)
                m_old = m_scr[...]
                m_new = jnp.maximum(m_old, jnp.max(sT, axis=0, keepdims=True))
                p = jnp.exp2(sT - m_new).astype(BF16)
                acc_scr[...] = jnp.exp2(m_old - m_new) * acc_scr[...] + _dot(v1T_ref[0, 0, :, pl.ds(ks, kc)], p)
                m_scr[...] = m_new
                return c2

            lax.fori_loop(0, nk, chunk, 0)
            acc = acc_scr[...]
            oT_scr[g] = acc[0:HEAD_DIM] / acc[HEAD_DIM:HEAD_DIM + 1]
            return carry

        lax.fori_loop(0, ATT_GROUP, head, 0)

    o_ref[0] = oT_scr[...].reshape(ATT_GROUP * HEAD_DIM, tq).T.astype(BF16)


def _attention(qT, k1, v1T, *, tq, kc, hpb):
    B, _, _, T = qT.shape
    n_keys = k1.shape[2]
    return pl.pallas_call(
        functools.partial(_attn_kernel, n_keys=n_keys, kc=kc, tq=tq, hpb=hpb),
        grid=(B, ATT_KV_HEADS, T // tq),
        in_specs=[
            pl.BlockSpec((1, ATT_GROUP, HEAD_DIM, tq), lambda b, h, i: (b, h, 0, i)),
            pl.BlockSpec((1, 1, n_keys, LANES), lambda b, h, i: (b, h, 0, 0)),
            pl.BlockSpec((1, 1, V_ROWS, n_keys), lambda b, h, i: (b, h, 0, 0)),
        ],
        out_specs=pl.BlockSpec((1, tq, ATT_GROUP * HEAD_DIM), lambda b, h, i: (b, i, h)),
        out_shape=jax.ShapeDtypeStruct((B, T, ATT_WIDTH), BF16),
        scratch_shapes=[
            pltpu.VMEM((ATT_GROUP, HEAD_DIM, tq), F32),
            pltpu.VMEM((ATT_GROUP, LANES, tq), BF16),
            pltpu.VMEM((8, LANES), F32),
            pltpu.VMEM((V_ROWS, tq), F32),
            pltpu.VMEM((1, tq), F32),
        ],
        compiler_params=pltpu.CompilerParams(
            dimension_semantics=("parallel", "parallel", "arbitrary"), vmem_limit_bytes=VMEM_LIMIT),
        name="attention",
    )(qT, k1, v1T)


def _mlstm_kernel(qf_ref, kTf_ref, vf_ref, qb_ref, kTb_ref, vb_ref, gcf_ref, gcb_ref, grf_ref, grb_ref,
                  s0_ref, m0_ref, hf_ref, hb_ref, sout_ref, mout_ref, s_scr, m_scr, *, nc):
    c = pl.program_id(1)

    @pl.when(c == 0)
    def _():
        s_scr[...] = s0_ref[0]
        m_scr[...] = m0_ref[0]

    ti = lax.broadcasted_iota(jnp.int32, (CHUNK, CHUNK), 0)
    si = lax.broadcasted_iota(jnp.int32, (CHUNK, CHUNK), 1)
    lower = si <= ti
    upper = si >= ti
    lower_b = lower.astype(BF16)
    upper_b = upper.astype(BF16)
    ones_v = jnp.ones((CHUNK, M_DIM), BF16)

    for d in range(2):
        q_ref, kT_ref, v_ref = (qf_ref, kTf_ref, vf_ref) if d == 0 else (qb_ref, kTb_ref, vb_ref)
        gc = (gcf_ref if d == 0 else gcb_ref)[0]
        gr = (grf_ref if d == 0 else grb_ref)[0]
        valid = lower if d == 0 else upper
        tri_col = lower_b if d == 0 else upper_b
        tri_row = upper_b if d == 0 else lower_b
        gc_hi = gc.astype(BF16)
        gc_lo = (gc - gc_hi.astype(F32)).astype(BF16)
        acol_all = _dot(tri_col, gc_hi) + _dot(tri_col, gc_lo)
        gr_hi = gr.astype(BF16)
        gr_lo = (gr - gr_hi.astype(F32)).astype(BF16)
        arow_all = _dot(gr_hi, tri_row) + _dot(gr_lo, tri_row)
        last = CHUNK - 1 if d == 0 else 0
        outs = []
        for hd in range(M_HEADS):
            j = M_HEADS * d + hd
            ci = 2 * M_HEADS * d + hd
            a_col = acol_all[:, ci + M_HEADS:ci + M_HEADS + 1]
            a_row = arow_all[ci + M_HEADS:ci + M_HEADS + 1, :]
            r_row = gr[ci:ci + 1, :] - a_row
            rb = jnp.where(valid, r_row, -jnp.inf)
            R_col = jnp.max(rb, axis=1, keepdims=True)
            dmat = jnp.exp(rb - R_col)
            qh = q_ref[0, hd]
            khT = kT_ref[0, hd]
            v1 = jnp.concatenate([v_ref[0, hd], ones_v], axis=1)
            s0 = (_dot(qh, khT) * dmat).astype(BF16)
            p = _dot(s0, v1)
            R_L = jnp.max(r_row, axis=1, keepdims=True)
            kw = (khT.astype(F32) * jnp.exp(r_row - R_L)).astype(BF16)
            u = _dot(kw, v1)
            m = m_scr[j][:, 0:1]
            st = s_scr[j]
            M_col = jnp.maximum(m, R_col)
            qc = _dot(qh, st.astype(BF16))
            tot = jnp.exp(R_col - M_col) * p + jnp.exp(m - M_col) * qc
            den = jnp.maximum(jnp.abs(tot), jnp.exp(-(a_col + M_col)))
            outs.append(tot[:, 0:M_DIM] / den[:, M_DIM:2 * M_DIM])
            M_L = jnp.maximum(m, R_L)
            s_scr[j] = jnp.exp(m - M_L) * st + jnp.exp(R_L - M_L) * u
            m_scr[j] = jnp.broadcast_to(a_row[:, last:last + 1] + M_L, (1, LANES))
        (hf_ref if d == 0 else hb_ref)[0] = jnp.concatenate(outs, axis=1)

    @pl.when(c == nc - 1)
    def _():
        sout_ref[0] = s_scr[...]
        mout_ref[0] = m_scr[...]


def _mlstm(qm, kmT, vm, gcol, grow, s0, m0):
    B, _, T, _ = qm.shape
    nc = T // CHUNK
    fwd3 = lambda b, c: (b, 0, c, 0)
    bwd3 = lambda b, c: (b, 0, nc - 1 - c, 0)
    fwdT = lambda b, c: (b, 0, 0, c)
    bwdT = lambda b, c: (b, 0, 0, nc - 1 - c)
    qspec = lambda im: pl.BlockSpec((1, M_HEADS, CHUNK, M_DIM), im)
    kspec = lambda im: pl.BlockSpec((1, M_HEADS, M_DIM, CHUNK), im)
    n_state = 2 * M_HEADS
    return pl.pallas_call(
        functools.partial(_mlstm_kernel, nc=nc),
        grid=(B, nc),
        in_specs=[
            qspec(fwd3), kspec(fwdT), qspec(fwd3),
            qspec(bwd3), kspec(bwdT), qspec(bwd3),
            pl.BlockSpec((1, CHUNK, GATE_PAD), lambda b, c: (b, c, 0)),
            pl.BlockSpec((1, CHUNK, GATE_PAD), lambda b, c: (b, nc - 1 - c, 0)),
            pl.BlockSpec((1, N_GATES, CHUNK), lambda b, c: (b, 0, c)),
            pl.BlockSpec((1, N_GATES, CHUNK), lambda b, c: (b, 0, nc - 1 - c)),
            pl.BlockSpec((1, n_state, M_DIM, LANES), lambda b, c: (b, 0, 0, 0)),
            pl.BlockSpec((1, n_state, 1, LANES), lambda b, c: (b, 0, 0, 0)),
        ],
        out_specs=[
            pl.BlockSpec((1, CHUNK, M_WIDTH), lambda b, c: (b, c, 0)),
            pl.BlockSpec((1, CHUNK, M_WIDTH), lambda b, c: (b, nc - 1 - c, 0)),
            pl.BlockSpec((1, n_state, M_DIM, LANES), lambda b, c: (b, 0, 0, 0)),
            pl.BlockSpec((1, n_state, 1, LANES), lambda b, c: (b, 0, 0, 0)),
        ],
        out_shape=[
            jax.ShapeDtypeStruct((B, T, M_WIDTH), F32),
            jax.ShapeDtypeStruct((B, T, M_WIDTH), F32),
            jax.ShapeDtypeStruct((B, n_state, M_DIM, LANES), F32),
            jax.ShapeDtypeStruct((B, n_state, 1, LANES), F32),
        ],
        scratch_shapes=[pltpu.VMEM((n_state, M_DIM, LANES), F32), pltpu.VMEM((n_state, 1, LANES), F32)],
        compiler_params=pltpu.CompilerParams(
            dimension_semantics=("parallel", "arbitrary"), vmem_limit_bytes=VMEM_LIMIT),
        name="mlstm",
    )(qm, kmT, vm, qm, kmT, vm, gcol, gcol, grow, grow, s0, m0)


def _post_kernel(att_ref, hf_ref, hb_ref, so_ref, go_ref, x_ref, mod_ref, wo_ref, l1g_ref, l1b_ref,
                 w1_ref, w2_ref, l2g_ref, l2b_ref, o_ref, *, ff_chunk):
    mod = mod_ref[0]
    g1 = mod[:, 2 * D_MODEL:3 * D_MODEL]
    sh2 = mod[:, 3 * D_MODEL:4 * D_MODEL]
    sc2 = mod[:, 4 * D_MODEL:5 * D_MODEL]
    g2 = mod[:, 5 * D_MODEL:6 * D_MODEL]
    m_out = (so_ref[0].astype(F32) * (hf_ref[0] + hb_ref[0])).astype(BF16)
    cat = jnp.concatenate([att_ref[0], m_out, go_ref[0]], axis=1)
    y = _dot(cat, wo_ref[...])
    x1 = _layer_norm(DEEPNORM_ALPHA * x_ref[0] + g1 * y, l1g_ref[...], l1b_ref[...])
    h2 = (x1 * (1.0 + sc2) + sh2).astype(BF16)
    f = None
    for j in range(D_FF // ff_chunk):
        a = jnp.maximum(_dot(h2, w1_ref[:, j * ff_chunk:(j + 1) * ff_chunk]), 0.0)
        part = _dot((a * a).astype(BF16), w2_ref[j * ff_chunk:(j + 1) * ff_chunk, :])
        f = part if f is None else f + part
    o_ref[0] = _layer_norm(DEEPNORM_ALPHA * x1 + g2 * f, l2g_ref[...], l2b_ref[...])


def _post(att, hf, hb, so, go, x, mod, mod_row, wo, l1g, l1b, w1, w2, l2g, l2b, *, tm, ff_chunk):
    B, T, _ = x.shape
    mod_idx = (lambda b, i: (b + mod_row, 0, 0)) if mod_row else (lambda b, i: (0, 0, 0))
    tok = lambda w: pl.BlockSpec((1, tm, w), lambda b, i: (b, i, 0))
    return pl.pallas_call(
        functools.partial(_post_kernel, ff_chunk=ff_chunk),
        grid=(B, T // tm),
        in_specs=[
            tok(ATT_WIDTH), tok(M_WIDTH), tok(M_WIDTH), tok(M_WIDTH), tok(G_WIDTH), tok(D_MODEL),
            pl.BlockSpec((1, 1, 6 * D_MODEL), mod_idx),
            _resident((D_MODEL, D_MODEL)), _resident((1, D_MODEL)), _resident((1, D_MODEL)),
            _resident((D_MODEL, D_FF)), _resident((D_FF, D_MODEL)),
            _resident((1, D_MODEL)), _resident((1, D_MODEL)),
        ],
        out_specs=tok(D_MODEL),
        out_shape=jax.ShapeDtypeStruct((B, T, D_MODEL), F32),
        compiler_params=pltpu.CompilerParams(
            dimension_semantics=("parallel", "parallel"), vmem_limit_bytes=VMEM_LIMIT),
        name="post",
    )(att, hf, hb, so, go, x, mod, wo, l1g, l1b, w1, w2, l2g, l2b)


def _rope_tables(T):
    rows = T // GRID_W
    r, cidx = jnp.meshgrid(jnp.arange(rows, dtype=F32), jnp.arange(GRID_W, dtype=F32), indexing='ij')
    n_freq = HEAD_DIM // 4
    inv = ROPE_BASE ** (-jnp.arange(n_freq, dtype=F32) / n_freq)
    ang_r = (r.reshape(-1)[:, None] * inv[None, :]).T
    ang_c = (cidx.reshape(-1)[:, None] * inv[None, :]).T
    cos = jnp.concatenate([jnp.cos(ang_r)] * 2 + [jnp.cos(ang_c)] * 2, axis=0)
    sin = jnp.concatenate([-jnp.sin(ang_r), jnp.sin(ang_r), -jnp.sin(ang_c), jnp.sin(ang_c)], axis=0)
    return cos, sin


def _with_ones_rows(vT):
    ones = jnp.ones(vT.shape[:-2] + (BF16_SUBLANES, vT.shape[-1]), vT.dtype)
    return jnp.concatenate([vT, ones], axis=-2)


def _pack_state(C, n, m):
    B = C.shape[0]
    ns = 2 * M_HEADS
    C = C.reshape(B, ns, M_DIM, M_DIM).astype(F32)
    n = jnp.broadcast_to(n.reshape(B, ns, M_DIM, 1).astype(F32), (B, ns, M_DIM, LANES - M_DIM))
    m = jnp.broadcast_to(m.reshape(B, ns, 1, 1).astype(F32), (B, ns, 1, LANES))
    return jnp.concatenate([C, n], axis=-1), m


def kernel(x_prompt, x_sample, c, cache_attn_k, cache_attn_v, state_mlstm_C, state_mlstm_n, state_mlstm_m,
           c_ctx, w_mod, b_mod, w_in, q_norm_g, k_norm_g, mlstm_gate_b, gmlp_ln_g, gmlp_ln_b, gmlp_ws,
           gmlp_bs, w_out, ln1_g, ln1_b, w_ff1, w_ff2, ln2_g, ln2_b):
    BP, TP, _ = x_prompt.shape
    BS, TS, _ = x_sample.shape
    n_mod = 16
    cvec = jnp.zeros((n_mod, D_MODEL), F32).at[0].set(c_ctx).at[1:1 + BS].set(c)
    mod_all = _modulation(cvec, w_mod, b_mod).reshape(DEPTH, n_mod, 1, 6 * D_MODEL)

    cos_s, sin_s = _rope_tables(TS)
    tm_p, tm_s = TP, 512

    xp, xs = x_prompt, x_sample
    ks_, vs_, Cs_, ns_, ms_ = [], [], [], [], []
    for l in range(DEPTH):
        wl = w_in[l]
        g0 = OFF_OM + M_WIDTH
        w_in_l = jnp.concatenate(
            [wl[:, :g0], wl[:, g0 + N_GATES:], wl[:, g0:g0 + N_GATES],
             jnp.zeros((D_MODEL, GATE_PAD - N_GATES), F32)], axis=1).astype(BF16)
        gate_b = jnp.zeros((1, GATE_PAD), F32).at[0, :N_GATES].set(mlstm_gate_b[l].reshape(-1))
        lng = gmlp_ln_g[l].reshape(1, G_WIDTH)
        lnb = gmlp_ln_b[l].reshape(1, G_WIDTH)
        ws = gmlp_ws[l].astype(BF16)
        bsx = jnp.repeat(gmlp_bs[l].T, G_CH, axis=1)
        wo = w_out[l].astype(BF16)
        w1 = w_ff1[l].astype(BF16)
        w2 = w_ff2[l].astype(BF16)
        l1g, l1b = ln1_g[l].reshape(1, -1), ln1_b[l].reshape(1, -1)
        l2g, l2b = ln2_g[l].reshape(1, -1), ln2_b[l].reshape(1, -1)
        mod = mod_all[l]

        def gains(tm):
            gq = jnp.broadcast_to((q_norm_g[l] * (HEAD_DIM ** -0.5 * LOG2_E))[:, None], (HEAD_DIM, tm))
            gk = jnp.broadcast_to(k_norm_g[l][:, None], (HEAD_DIM, tm))
            return gq, gk

        gq, gk = gains(tm_p)
        (qT, k, v1T, qm, kmT, vm, so, go, gcol, grow, k32, v32) = _inproj(
            xp, mod, 0, w_in_l, gq, gk, None, None, gate_b, lng, lnb, ws, bsx,
            rope=False, emit_kv32=True, tm=tm_p)
        att = _attention(qT, k, v1T, tq=TP, kc=TP, hpb=1)
        zero_state = _pack_state(jnp.zeros((BP, 2, M_HEADS, M_DIM, M_DIM), F32),
                                 jnp.zeros((BP, 2, M_HEADS, M_DIM), F32),
                                 jnp.zeros((BP, 2, M_HEADS), F32))
        hf, hb, s_fin, m_fin = _mlstm(qm, kmT, vm, gcol, grow, *zero_state)
        xp = _post(att, hf, hb, so, go, xp, mod, 0, wo, l1g, l1b, w1, w2, l2g, l2b, tm=tm_p, ff_chunk=1024)
        ks_.append(k32.reshape(BP, TP, ATT_KV_HEADS, HEAD_DIM))
        vs_.append(v32.reshape(BP, TP, ATT_KV_HEADS, HEAD_DIM))
        Cs_.append(s_fin[..., :M_DIM].reshape(BP, 2, M_HEADS, M_DIM, M_DIM))
        ns_.append(s_fin[..., M_DIM].reshape(BP, 2, M_HEADS, M_DIM))
        ms_.append(m_fin[:, :, 0, 0].reshape(BP, 2, M_HEADS))

        gq, gk = gains(tm_s)
        (qT, k, v1T, qm, kmT, vm, so, go, gcol, grow) = _inproj(
            xs, mod, 1, w_in_l, gq, gk, cos_s, sin_s, gate_b, lng, lnb, ws, bsx,
            rope=True, emit_kv32=False, tm=tm_s)
        k_c = jnp.transpose(cache_attn_k[:, l], (0, 2, 1, 3)).astype(BF16)
        ones_col = jnp.zeros(k_c.shape[:-1] + (LANES - HEAD_DIM,), BF16).at[..., 0].set(1.0)
        k_c = jnp.concatenate([k_c, ones_col], axis=-1)
        vT_c = _with_ones_rows(jnp.transpose(cache_attn_v[:, l], (0, 2, 3, 1)).astype(BF16))
        k_all = jnp.concatenate([k_c, k], axis=2)
        v1T_all = jnp.concatenate([vT_c, v1T], axis=3)
        att = _attention(qT, k_all, v1T_all, tq=ATT_TQ, kc=ATT_KC, hpb=ATT_HPB)
        state = _pack_state(state_mlstm_C[:, l], state_mlstm_n[:, l], state_mlstm_m[:, l])
        hf, hb, _, _ = _mlstm(qm, kmT, vm, gcol, grow, *state)
        xs = _post(att, hf, hb, so, go, xs, mod, 1, wo, l1g, l1b, w1, w2, l2g, l2b, tm=tm_s, ff_chunk=1024)

    return (xp, xs, jnp.stack(ks_, axis=1), jnp.stack(vs_, axis=1), jnp.stack(Cs_, axis=1),
            jnp.stack(ns_, axis=1), jnp.stack(ms_, axis=1))
```

```python
import functools

import jax
import jax.numpy as jnp
from jax import lax
from jax.experimental import pallas as pl
from jax.experimental.pallas import tpu as pltpu

D_MODEL = 1024
DEPTH = 2
GRID_W = 64
HEAD_DIM = 64
ATT_HEADS = 8
ATT_KV_HEADS = 2
ATT_GROUP = ATT_HEADS // ATT_KV_HEADS
ATT_WIDTH = ATT_HEADS * HEAD_DIM
KV_WIDTH = ATT_KV_HEADS * HEAD_DIM
M_HEADS = 4
M_DIM = 64
M_WIDTH = M_HEADS * M_DIM
G_GROUPS = 4
G_WIDTH = D_MODEL - ATT_WIDTH - M_WIDTH
G_CH = G_WIDTH // G_GROUPS
CHUNK = 128
D_FF = 4 * D_MODEL
ROPE_BASE = 10000.0
EPS = 1e-6
DEEPNORM_ALPHA = (2 * DEPTH) ** 0.25
N_GATES = 4 * M_HEADS

LANES = 128
BF16_SUBLANES = 16
GATE_PAD = LANES
OFF_QA = 0
OFF_KA = OFF_QA + ATT_WIDTH
OFF_VA = OFF_KA + KV_WIDTH
OFF_QM = OFF_VA + KV_WIDTH
OFF_KM = OFF_QM + M_WIDTH
OFF_VM = OFF_KM + M_WIDTH
OFF_OM = OFF_VM + M_WIDTH
OFF_UG = OFF_OM + M_WIDTH
OFF_VG = OFF_UG + G_WIDTH
OFF_GM = OFF_VG + G_WIDTH
PROJ_PAD = OFF_GM + GATE_PAD
V_ROWS = HEAD_DIM + BF16_SUBLANES
VMEM_LIMIT = 56 * 1024 * 1024

ATT_TQ, ATT_KC, ATT_HPB = 512, 512, 2
MLSTM_BATCH = 2
LOG2_E = 1.4426950408889634
BOUND_MARGIN = 1.0 + 2.0 ** -6
MAX_FAST_SHIFT = 60.0

F32 = jnp.float32
BF16 = jnp.bfloat16


def _dot(a, b):
    return jnp.dot(a, b, preferred_element_type=F32)


def _resident(shape):
    nd = len(shape)
    return pl.BlockSpec(shape, lambda *_: (0,) * nd, pipeline_mode=pl.Buffered(1))


def _sigmoid(x):
    return 1.0 / (1.0 + jnp.exp(-x))


def _log_sigmoid(x):
    return jnp.minimum(x, 0.0) - jnp.log1p(jnp.exp(-jnp.abs(x)))


def _layer_norm(x, g, b):
    mu = jnp.mean(x, axis=-1, keepdims=True)
    xc = x - mu
    var = jnp.mean(xc * xc, axis=-1, keepdims=True)
    return xc * lax.rsqrt(var + EPS) * g + b


def _mod_kernel(c_ref, w_ref, b_ref, o_ref):
    c = c_ref[...]
    s = (c * _sigmoid(c)).astype(BF16)
    o_ref[0] = _dot(s, w_ref[0].astype(BF16)) + b_ref[0]


def _modulation(cvec, w_mod, b_mod):
    n = cvec.shape[0]
    bn = 1536
    return pl.pallas_call(
        _mod_kernel,
        grid=(DEPTH, 6 * D_MODEL // bn),
        in_specs=[
            pl.BlockSpec((n, D_MODEL), lambda l, j: (0, 0)),
            pl.BlockSpec((1, D_MODEL, bn), lambda l, j: (l, 0, j)),
            pl.BlockSpec((1, 1, bn), lambda l, j: (l, 0, j)),
        ],
        out_specs=pl.BlockSpec((1, n, bn), lambda l, j: (l, 0, j)),
        out_shape=jax.ShapeDtypeStruct((DEPTH, n, 6 * D_MODEL), F32),
        compiler_params=pltpu.CompilerParams(
            dimension_semantics=("arbitrary", "arbitrary"), vmem_limit_bytes=VMEM_LIMIT),
        name="modulation",
    )(cvec, w_mod, b_mod.reshape(DEPTH, 1, 6 * D_MODEL))


def _norm_rope_heads(zT, gain, cos, sin, n_heads):
    outs = []
    for hd in range(n_heads):
        t = zT[hd * HEAD_DIM:(hd + 1) * HEAD_DIM, :]
        ms = jnp.mean(t * t, axis=0, keepdims=True)
        y = t * lax.rsqrt(ms + EPS) * gain
        if cos is not None:
            q = HEAD_DIM // 4
            sw = jnp.concatenate([y[q:2 * q], y[0:q], y[3 * q:4 * q], y[2 * q:3 * q]], axis=0)
            y = y * cos + sw * sin
        outs.append(y)
    return outs


def _inproj_kernel(*refs, rope, emit_kv32, tm):
    it = iter(refs)
    x_ref, mod_ref, w_ref, gq_ref, gk_ref = (next(it) for _ in range(5))
    cos_ref, sin_ref = (next(it), next(it)) if rope else (None, None)
    gb_ref, lng_ref, lnb_ref, ws_ref, bsx_ref = (next(it) for _ in range(5))
    qT_ref, k_ref, vT_ref, qmT_ref, km_ref, vmT_ref, so_ref, go_ref, grow_ref = (
        next(it) for _ in range(9))
    k32_ref, v32_ref = (next(it), next(it)) if emit_kv32 else (None, None)

    x = x_ref[0]
    mod = mod_ref[0]
    sh1 = mod[:, 0:D_MODEL]
    sc1 = mod[:, D_MODEL:2 * D_MODEL]
    h = (x * (1.0 + sc1) + sh1).astype(BF16)

    cos = cos_ref[...] if rope else None
    sin = sin_ref[...] if rope else None

    zq = _dot(h, w_ref[:, OFF_QA:OFF_QA + ATT_WIDTH])
    q_heads = _norm_rope_heads(zq.T, gq_ref[...], cos, sin, ATT_HEADS)
    for hd in range(ATT_HEADS):
        qT_ref[0, hd] = q_heads[hd].astype(BF16)

    zk = _dot(h, w_ref[:, OFF_KA:OFF_KA + KV_WIDTH])
    k_heads = _norm_rope_heads(zk.T, gk_ref[...], cos, sin, ATT_KV_HEADS)
    kk = jnp.concatenate(k_heads, axis=0).T
    ones_col = (lax.broadcasted_iota(jnp.int32, (tm, LANES - HEAD_DIM), 1) == 0).astype(F32)
    for hd in range(ATT_KV_HEADS):
        k_ref[0, hd] = jnp.concatenate(
            [kk[:, hd * HEAD_DIM:(hd + 1) * HEAD_DIM], ones_col], axis=1).astype(BF16)
    zv = _dot(h, w_ref[:, OFF_VA:OFF_VA + KV_WIDTH])
    vT = zv.T
    ones_rows = jnp.ones((BF16_SUBLANES, tm), BF16)
    for hd in range(ATT_KV_HEADS):
        vT_ref[0, hd] = jnp.concatenate(
            [vT[hd * HEAD_DIM:(hd + 1) * HEAD_DIM].astype(BF16), ones_rows], axis=0)
    if emit_kv32:
        k32_ref[0] = kk
        v32_ref[0] = zv

    zqmT = _dot(h, w_ref[:, OFF_QM:OFF_QM + M_WIDTH]).T
    zkm = _dot(h, w_ref[:, OFF_KM:OFF_KM + M_WIDTH]) * (M_DIM ** -0.5)
    zvmT = _dot(h, w_ref[:, OFF_VM:OFF_VM + M_WIDTH]).T
    for hd in range(M_HEADS):
        qmT_ref[0, hd] = zqmT[hd * M_DIM:(hd + 1) * M_DIM, :].astype(BF16)
        km_ref[0, hd] = zkm[:, hd * M_DIM:(hd + 1) * M_DIM].astype(BF16)
        vmT_ref[0, hd] = zvmT[hd * M_DIM:(hd + 1) * M_DIM, :].astype(BF16)
    zo = _dot(h, w_ref[:, OFF_OM:OFF_OM + M_WIDTH])
    so_ref[0] = _sigmoid(zo).astype(BF16)

    zg = _dot(h, w_ref[:, OFF_GM:OFF_GM + GATE_PAD]) + gb_ref[...]
    lane = lax.broadcasted_iota(jnp.int32, zg.shape, 1)
    is_forget = (lane % (2 * M_HEADS)) >= M_HEADS
    grow_ref[0] = jnp.where(is_forget, _log_sigmoid(zg), zg).T[0:N_GATES, :]

    zu = _dot(h, w_ref[:, OFF_UG:OFF_UG + G_WIDTH])
    zvg = _dot(h, w_ref[:, OFF_VG:OFF_VG + G_WIDTH])
    vn = _layer_norm(zvg, lng_ref[...], lnb_ref[...])
    group = lax.broadcasted_iota(jnp.int32, (CHUNK, G_WIDTH), 1) // G_CH
    bsx = bsx_ref[...]
    s_chunks = []
    for j in range(tm // CHUNK):
        vc = vn[j * CHUNK:(j + 1) * CHUNK, :]
        acc = bsx
        for g in range(G_GROUPS):
            acc = acc + _dot(ws_ref[g], jnp.where(group == g, vc, 0.0).astype(BF16))
        s_chunks.append(acc)
    s = jnp.concatenate(s_chunks, axis=0) if len(s_chunks) > 1 else s_chunks[0]
    go_ref[0] = (zu * s).astype(BF16)


def _inproj(x, mod, mod_row, w_in, gq, gk, cos, sin, gate_b, lng, lnb, ws, bsx, *, rope, emit_kv32, tm):
    B, T, _ = x.shape
    nt = T // tm
    mod_idx = (lambda b, i: (b + mod_row, 0, 0)) if mod_row else (lambda b, i: (0, 0, 0))
    in_specs = [
        pl.BlockSpec((1, tm, D_MODEL), lambda b, i: (b, i, 0)),
        pl.BlockSpec((1, 1, 6 * D_MODEL), mod_idx),
        _resident((D_MODEL, PROJ_PAD)),
        _resident((HEAD_DIM, tm)),
        _resident((HEAD_DIM, tm)),
    ]
    args = [x, mod, w_in, gq, gk]
    if rope:
        in_specs += [pl.BlockSpec((HEAD_DIM, tm), lambda b, i: (0, i))] * 2
        args += [cos, sin]
    in_specs += [
        _resident((1, GATE_PAD)),
        _resident((1, G_WIDTH)),
        _resident((1, G_WIDTH)),
        _resident((G_GROUPS, CHUNK, CHUNK)),
        _resident((CHUNK, G_WIDTH)),
    ]
    args += [gate_b, lng, lnb, ws, bsx]
    out_shape = [
        jax.ShapeDtypeStruct((B, ATT_HEADS, HEAD_DIM, T), BF16),
        jax.ShapeDtypeStruct((B, ATT_KV_HEADS, T, LANES), BF16),
        jax.ShapeDtypeStruct((B, ATT_KV_HEADS, V_ROWS, T), BF16),
        jax.ShapeDtypeStruct((B, M_HEADS, M_DIM, T), BF16),
        jax.ShapeDtypeStruct((B, M_HEADS, T, M_DIM), BF16),
        jax.ShapeDtypeStruct((B, M_HEADS, M_DIM, T), BF16),
        jax.ShapeDtypeStruct((B, T, M_WIDTH), BF16),
        jax.ShapeDtypeStruct((B, T, G_WIDTH), BF16),
        jax.ShapeDtypeStruct((B, N_GATES, T), F32),
    ]
    out_specs = [
        pl.BlockSpec((1, ATT_HEADS, HEAD_DIM, tm), lambda b, i: (b, 0, 0, i)),
        pl.BlockSpec((1, ATT_KV_HEADS, tm, LANES), lambda b, i: (b, 0, i, 0)),
        pl.BlockSpec((1, ATT_KV_HEADS, V_ROWS, tm), lambda b, i: (b, 0, 0, i)),
        pl.BlockSpec((1, M_HEADS, M_DIM, tm), lambda b, i: (b, 0, 0, i)),
        pl.BlockSpec((1, M_HEADS, tm, M_DIM), lambda b, i: (b, 0, i, 0)),
        pl.BlockSpec((1, M_HEADS, M_DIM, tm), lambda b, i: (b, 0, 0, i)),
        pl.BlockSpec((1, tm, M_WIDTH), lambda b, i: (b, i, 0)),
        pl.BlockSpec((1, tm, G_WIDTH), lambda b, i: (b, i, 0)),
        pl.BlockSpec((1, N_GATES, tm), lambda b, i: (b, 0, i)),
    ]
    if emit_kv32:
        out_shape += [jax.ShapeDtypeStruct((B, T, KV_WIDTH), F32)] * 2
        out_specs += [pl.BlockSpec((1, tm, KV_WIDTH), lambda b, i: (b, i, 0))] * 2
    return pl.pallas_call(
        functools.partial(_inproj_kernel, rope=rope, emit_kv32=emit_kv32, tm=tm),
        grid=(B, nt),
        in_specs=in_specs,
        out_specs=out_specs,
        out_shape=out_shape,
        compiler_params=pltpu.CompilerParams(
            dimension_semantics=("parallel", "parallel"), vmem_limit_bytes=VMEM_LIMIT),
        name="inproj",
    )(*args)


def _attn_kernel(qT_ref, k1_ref, v1T_ref, o_ref, oT_scr, q1_scr, kmax_scr, acc_scr, m_scr,
                 *, n_keys, kc, tq, hpb):
    nk = n_keys // kc
    i = pl.program_id(2)

    @pl.when(i == 0)
    def _():
        kf = k1_ref[0, 0].astype(F32)
        n2 = jnp.sum(kf * kf, axis=1, keepdims=True) - 1.0
        kmax_scr[...] = jnp.broadcast_to(jnp.max(n2, axis=0, keepdims=True), kmax_scr.shape)

    kmax2 = kmax_scr[0:1, 0:1]
    row0 = lax.broadcasted_iota(jnp.int32, (BF16_SUBLANES, tq), 0) == 0
    pad = jnp.zeros((LANES - HEAD_DIM - BF16_SUBLANES, tq), BF16)
    cmax = None
    for g in range(ATT_GROUP):
        qT = qT_ref[0, g]
        qf = qT.astype(F32)
        c = jnp.sqrt(jnp.sum(qf * qf, axis=0, keepdims=True) * kmax2) * BOUND_MARGIN
        shift = jnp.where(row0, -c, 0.0).astype(BF16)
        q1_scr[g] = jnp.concatenate([qT, shift, pad], axis=0)
        cg = jnp.max(c)
        cmax = cg if cmax is None else jnp.maximum(cmax, cg)
    fast = cmax <= MAX_FAST_SHIFT

    @pl.when(fast)
    def _():
        def heads(gi, carry):
            gs = [gi * hpb + a for a in range(hpb)]
            q1s = [q1_scr[g] for g in gs]

            def scores(j):
                kblk = k1_ref[0, 0, j * kc:(j + 1) * kc, :]
                return [_dot(kblk, q1s[a]) for a in range(hpb)]

            accs = [None] * hpb
            s_next = scores(0)
            for j in range(nk):
                s_cur = s_next
                if j + 1 < nk:
                    s_next = scores(j + 1)
                vblk = v1T_ref[0, 0, :, j * kc:(j + 1) * kc]
                for a in range(hpb):
                    pv = _dot(vblk, jnp.exp2(s_cur[a]).astype(BF16))
                    accs[a] = pv if j == 0 else accs[a] + pv
            for a in range(hpb):
                oT_scr[gs[a]] = accs[a][0:HEAD_DIM] / accs[a][HEAD_DIM:HEAD_DIM + 1]
            return carry

        lax.fori_loop(0, ATT_GROUP // hpb, heads, 0)

    @pl.when(jnp.logical_not(fast))
    def _():
        def head(g, carry):
            qT = qT_ref[0, g]
            m_scr[...] = jnp.full(m_scr.shape, -jnp.inf, F32)
            acc_scr[...] = jnp.zeros(acc_scr.shape, F32)

            def chunk(j, c2):
                ks = pl.multiple_of(j * kc, kc)
                sT = _dot(k1_ref[0, 0, pl.ds(ks, kc), 0:HEAD_DIM], qT)
                m_old = m_scr[...]
                m_new = jnp.maximum(m_old, jnp.max(sT, axis=0, keepdims=True))
                p = jnp.exp2(sT - m_new).astype(BF16)
                acc_scr[...] = jnp.exp2(m_old - m_new) * acc_scr[...] + _dot(v1T_ref[0, 0, :, pl.ds(ks, kc)], p)
                m_scr[...] = m_new
                return c2

            lax.fori_loop(0, nk, chunk, 0)
            acc = acc_scr[...]
            oT_scr[g] = acc[0:HEAD_DIM] / acc[HEAD_DIM:HEAD_DIM + 1]
            return carry

        lax.fori_loop(0, ATT_GROUP, head, 0)

    o_ref[0] = oT_scr[...].reshape(ATT_GROUP * HEAD_DIM, tq).T.astype(BF16)


def _attention(qT, k1, v1T, *, tq, kc, hpb):
    B, _, _, T = qT.shape
    n_keys = k1.shape[2]
    return pl.pallas_call(
        functools.partial(_attn_kernel, n_keys=n_keys, kc=kc, tq=tq, hpb=hpb),
        grid=(B, ATT_KV_HEADS, T // tq),
        in_specs=[
            pl.BlockSpec((1, ATT_GROUP, HEAD_DIM, tq), lambda b, h, i: (b, h, 0, i)),
            pl.BlockSpec((1, 1, n_keys, LANES), lambda b, h, i: (b, h, 0, 0)),
            pl.BlockSpec((1, 1, V_ROWS, n_keys), lambda b, h, i: (b, h, 0, 0)),
        ],
        out_specs=pl.BlockSpec((1, tq, ATT_GROUP * HEAD_DIM), lambda b, h, i: (b, i, h)),
        out_shape=jax.ShapeDtypeStruct((B, T, ATT_WIDTH), BF16),
        scratch_shapes=[
            pltpu.VMEM((ATT_GROUP, HEAD_DIM, tq), F32),
            pltpu.VMEM((ATT_GROUP, LANES, tq), BF16),
            pltpu.VMEM((8, LANES), F32),
            pltpu.VMEM((V_ROWS, tq), F32),
            pltpu.VMEM((1, tq), F32),
        ],
        compiler_params=pltpu.CompilerParams(
            dimension_semantics=("parallel", "parallel", "arbitrary"), vmem_limit_bytes=VMEM_LIMIT),
        name="attention",
    )(qT, k1, v1T)


def _mlstm_kernel(qTf_ref, kf_ref, vTf_ref, qTb_ref, kb_ref, vTb_ref, grf_ref, grb_ref,
                  s0_ref, m0_ref, hf_ref, hb_ref, sout_ref, mout_ref, s_scr, m_scr, *, nc, bb):
    c = pl.program_id(1)
    n_state = 2 * M_HEADS

    @pl.when(c == 0)
    def _():
        for bi in range(bb):
            s_scr[bi * n_state:(bi + 1) * n_state] = s0_ref[bi]
            m_scr[bi * n_state:(bi + 1) * n_state] = m0_ref[bi]

    si = lax.broadcasted_iota(jnp.int32, (CHUNK, CHUNK), 0)
    ti = lax.broadcasted_iota(jnp.int32, (CHUNK, CHUNK), 1)
    ones_v = jnp.ones((M_DIM, CHUNK), BF16)
    zrows = jnp.zeros((CHUNK - M_HEADS, CHUNK), F32)

    dirs = [(bi, d) for bi in range(bb) for d in range(2)]
    valids = [(si <= ti), (si >= ti)]
    gates = []
    for bi, d in dirs:
        gr = (grf_ref if d == 0 else grb_ref)[bi]
        tri = valids[d].astype(BF16)
        gr_hi = gr.astype(BF16)
        gr_lo = (gr - gr_hi.astype(F32)).astype(BF16)
        gates.append((gr, _dot(gr_hi, tri) + _dot(gr_lo, tri)))

    pre = []
    for bi, d in dirs:
        qT_ref, k_ref, vT_ref = (qTf_ref, kf_ref, vTf_ref) if d == 0 else (qTb_ref, kb_ref, vTb_ref)
        for hd in range(M_HEADS):
            qT = qT_ref[bi, hd]
            kh = k_ref[bi, hd]
            v1T = jnp.concatenate([vT_ref[bi, hd], ones_v], axis=0)
            pre.append(dict(qT=qT, kh=kh, v1T=v1T, kq=_dot(kh, qT)))

    for (bi, d), (gr, arow_all) in zip(dirs, gates):
        g0 = 2 * M_HEADS * d
        a_rows = arow_all[g0 + M_HEADS:g0 + 2 * M_HEADS, :]
        r_rows = gr[g0:g0 + M_HEADS, :] - a_rows
        r_cols = jnp.concatenate([r_rows, zrows], axis=0).T
        last = CHUNK - 1 if d == 0 else 0
        for hd in range(M_HEADS):
            e = pre[(2 * bi + d) * M_HEADS + hd]
            a_row = a_rows[hd:hd + 1, :]
            x = jnp.where(valids[d], r_cols[:, hd:hd + 1], -jnp.inf)
            R_row = jnp.max(x, axis=0, keepdims=True)
            R_L = jnp.broadcast_to(R_row[:, last:last + 1], (1, LANES))
            wk = jnp.exp(r_rows[hd:hd + 1, :] - R_L)
            e.update(a_row=a_row, R_row=R_row, R_L=R_L, a_L=a_row[:, last:last + 1],
                     kqd=e["kq"] * jnp.exp(x - R_row),
                     uT=_dot((e["v1T"].astype(F32) * wk).astype(BF16), e["kh"]))

    outs = []
    for j, e in enumerate(pre):
        m = m_scr[j]
        st = s_scr[j]
        M_row = jnp.maximum(m, e["R_row"])
        s0T = (e["kqd"] * jnp.exp(e["R_row"] - M_row)).astype(BF16)
        qs = (e["qT"].astype(F32) * jnp.exp(m - M_row)).astype(BF16)
        lhs = jnp.concatenate([e["v1T"], st.astype(BF16)], axis=1)
        tot = _dot(lhs, jnp.concatenate([s0T, qs], axis=0))
        den = jnp.maximum(jnp.abs(tot[M_DIM:2 * M_DIM]), jnp.exp(-(e["a_row"] + M_row)))
        outs.append(tot[0:M_DIM] / den)
        M_L = jnp.maximum(m, e["R_L"])
        s_scr[j] = jnp.exp(m - M_L)[:, 0:M_DIM] * st + jnp.exp(e["R_L"] - M_L)[:, 0:M_DIM] * e["uT"]
        m_scr[j] = jnp.broadcast_to(e["a_L"], (1, LANES)) + M_L
    for bi in range(bb):
        o = bi * n_state
        hf_ref[bi] = jnp.concatenate(outs[o:o + M_HEADS], axis=0).T
        hb_ref[bi] = jnp.concatenate(outs[o + M_HEADS:o + n_state], axis=0).T

    @pl.when(c == nc - 1)
    def _():
        for bi in range(bb):
            sout_ref[bi] = s_scr[bi * n_state:(bi + 1) * n_state]
            mout_ref[bi] = m_scr[bi * n_state:(bi + 1) * n_state]


def _mlstm(qmT, km, vmT, grow, s0, m0):
    B, _, _, T = qmT.shape
    nc = T // CHUNK
    fwd3 = lambda b, c: (b, 0, c, 0)
    bwd3 = lambda b, c: (b, 0, nc - 1 - c, 0)
    fwdT = lambda b, c: (b, 0, 0, c)
    bwdT = lambda b, c: (b, 0, 0, nc - 1 - c)
    bb = MLSTM_BATCH
    rspec = lambda im: pl.BlockSpec((bb, M_HEADS, CHUNK, M_DIM), im)
    tspec = lambda im: pl.BlockSpec((bb, M_HEADS, M_DIM, CHUNK), im)
    n_state = 2 * M_HEADS
    return pl.pallas_call(
        functools.partial(_mlstm_kernel, nc=nc, bb=bb),
        grid=(B // bb, nc),
        in_specs=[
            tspec(fwdT), rspec(fwd3), tspec(fwdT),
            tspec(bwdT), rspec(bwd3), tspec(bwdT),
            pl.BlockSpec((bb, N_GATES, CHUNK), lambda b, c: (b, 0, c)),
            pl.BlockSpec((bb, N_GATES, CHUNK), lambda b, c: (b, 0, nc - 1 - c)),
            pl.BlockSpec((bb, n_state, 2 * M_DIM, M_DIM), lambda b, c: (b, 0, 0, 0)),
            pl.BlockSpec((bb, n_state, 1, LANES), lambda b, c: (b, 0, 0, 0)),
        ],
        out_specs=[
            pl.BlockSpec((bb, CHUNK, M_WIDTH), lambda b, c: (b, c, 0)),
            pl.BlockSpec((bb, CHUNK, M_WIDTH), lambda b, c: (b, nc - 1 - c, 0)),
            pl.BlockSpec((bb, n_state, 2 * M_DIM, M_DIM), lambda b, c: (b, 0, 0, 0)),
            pl.BlockSpec((bb, n_state, 1, LANES), lambda b, c: (b, 0, 0, 0)),
        ],
        out_shape=[
            jax.ShapeDtypeStruct((B, T, M_WIDTH), F32),
            jax.ShapeDtypeStruct((B, T, M_WIDTH), F32),
            jax.ShapeDtypeStruct((B, n_state, 2 * M_DIM, M_DIM), F32),
            jax.ShapeDtypeStruct((B, n_state, 1, LANES), F32),
        ],
        scratch_shapes=[pltpu.VMEM((bb * n_state, 2 * M_DIM, M_DIM), F32),
                        pltpu.VMEM((bb * n_state, 1, LANES), F32)],
        compiler_params=pltpu.CompilerParams(
            dimension_semantics=("parallel", "arbitrary"), vmem_limit_bytes=VMEM_LIMIT),
        name="mlstm",
    )(qmT, km, vmT, qmT, km, vmT, grow, grow, s0, m0)


def _post_kernel(att_ref, hf_ref, hb_ref, so_ref, go_ref, x_ref, mod_ref, wo_ref, l1g_ref, l1b_ref,
                 w1_ref, w2_ref, l2g_ref, l2b_ref, o_ref, *, ff_chunk):
    mod = mod_ref[0]
    g1 = mod[:, 2 * D_MODEL:3 * D_MODEL]
    sh2 = mod[:, 3 * D_MODEL:4 * D_MODEL]
    sc2 = mod[:, 4 * D_MODEL:5 * D_MODEL]
    g2 = mod[:, 5 * D_MODEL:6 * D_MODEL]
    m_out = (so_ref[0].astype(F32) * (hf_ref[0] + hb_ref[0])).astype(BF16)
    cat = jnp.concatenate([att_ref[0], m_out, go_ref[0]], axis=1)
    y = _dot(cat, wo_ref[...])
    x1 = _layer_norm(DEEPNORM_ALPHA * x_ref[0] + g1 * y, l1g_ref[...], l1b_ref[...])
    h2 = (x1 * (1.0 + sc2) + sh2).astype(BF16)
    f = None
    for j in range(D_FF // ff_chunk):
        a = jnp.maximum(_dot(h2, w1_ref[:, j * ff_chunk:(j + 1) * ff_chunk]), 0.0)
        part = _dot((a * a).astype(BF16), w2_ref[j * ff_chunk:(j + 1) * ff_chunk, :])
        f = part if f is None else f + part
    o_ref[0] = _layer_norm(DEEPNORM_ALPHA * x1 + g2 * f, l2g_ref[...], l2b_ref[...])


def _post(att, hf, hb, so, go, x, mod, mod_row, wo, l1g, l1b, w1, w2, l2g, l2b, *, tm, ff_chunk):
    B, T, _ = x.shape
    mod_idx = (lambda b, i: (b + mod_row, 0, 0)) if mod_row else (lambda b, i: (0, 0, 0))
    tok = lambda w: pl.BlockSpec((1, tm, w), lambda b, i: (b, i, 0))
    return pl.pallas_call(
        functools.partial(_post_kernel, ff_chunk=ff_chunk),
        grid=(B, T // tm),
        in_specs=[
            tok(ATT_WIDTH), tok(M_WIDTH), tok(M_WIDTH), tok(M_WIDTH), tok(G_WIDTH), tok(D_MODEL),
            pl.BlockSpec((1, 1, 6 * D_MODEL), mod_idx),
            _resident((D_MODEL, D_MODEL)), _resident((1, D_MODEL)), _resident((1, D_MODEL)),
            _resident((D_MODEL, D_FF)), _resident((D_FF, D_MODEL)),
            _resident((1, D_MODEL)), _resident((1, D_MODEL)),
        ],
        out_specs=tok(D_MODEL),
        out_shape=jax.ShapeDtypeStruct((B, T, D_MODEL), F32),
        compiler_params=pltpu.CompilerParams(
            dimension_semantics=("parallel", "parallel"), vmem_limit_bytes=VMEM_LIMIT),
        name="post",
    )(att, hf, hb, so, go, x, mod, wo, l1g, l1b, w1, w2, l2g, l2b)


def _rope_tables(T):
    rows = T // GRID_W
    r, cidx = jnp.meshgrid(jnp.arange(rows, dtype=F32), jnp.arange(GRID_W, dtype=F32), indexing='ij')
    n_freq = HEAD_DIM // 4
    inv = ROPE_BASE ** (-jnp.arange(n_freq, dtype=F32) / n_freq)
    ang_r = (r.reshape(-1)[:, None] * inv[None, :]).T
    ang_c = (cidx.reshape(-1)[:, None] * inv[None, :]).T
    cos = jnp.concatenate([jnp.cos(ang_r)] * 2 + [jnp.cos(ang_c)] * 2, axis=0)
    sin = jnp.concatenate([-jnp.sin(ang_r), jnp.sin(ang_r), -jnp.sin(ang_c), jnp.sin(ang_c)], axis=0)
    return cos, sin


def _with_ones_rows(vT):
    ones = jnp.ones(vT.shape[:-2] + (BF16_SUBLANES, vT.shape[-1]), vT.dtype)
    return jnp.concatenate([vT, ones], axis=-2)


def _pack_state(C, n, m):
    B = C.shape[0]
    ns = 2 * M_HEADS
    CT = jnp.swapaxes(C.reshape(B, ns, M_DIM, M_DIM), -1, -2).astype(F32)
    n = jnp.broadcast_to(n.reshape(B, ns, 1, M_DIM).astype(F32), (B, ns, M_DIM, M_DIM))
    m = jnp.broadcast_to(m.reshape(B, ns, 1, 1).astype(F32), (B, ns, 1, LANES))
    return jnp.concatenate([CT, n], axis=-2), m


def kernel(x_prompt, x_sample, c, cache_attn_k, cache_attn_v, state_mlstm_C, state_mlstm_n, state_mlstm_m,
           c_ctx, w_mod, b_mod, w_in, q_norm_g, k_norm_g, mlstm_gate_b, gmlp_ln_g, gmlp_ln_b, gmlp_ws,
           gmlp_bs, w_out, ln1_g, ln1_b, w_ff1, w_ff2, ln2_g, ln2_b):
    BP, TP, _ = x_prompt.shape
    BS, TS, _ = x_sample.shape
    n_mod = 16
    cvec = jnp.zeros((n_mod, D_MODEL), F32).at[0].set(c_ctx).at[1:1 + BS].set(c)
    mod_all = _modulation(cvec, w_mod, b_mod).reshape(DEPTH, n_mod, 1, 6 * D_MODEL)

    cos_s, sin_s = _rope_tables(TS)
    tm_p, tm_s = TP, 512

    xp, xs = x_prompt, x_sample
    ks_, vs_, Cs_, ns_, ms_ = [], [], [], [], []
    for l in range(DEPTH):
        wl = w_in[l]
        g0 = OFF_OM + M_WIDTH
        w_in_l = jnp.concatenate(
            [wl[:, :g0], wl[:, g0 + N_GATES:], wl[:, g0:g0 + N_GATES],
             jnp.zeros((D_MODEL, GATE_PAD - N_GATES), F32)], axis=1).astype(BF16)
        gate_b = jnp.zeros((1, GATE_PAD), F32).at[0, :N_GATES].set(mlstm_gate_b[l].reshape(-1))
        lng = gmlp_ln_g[l].reshape(1, G_WIDTH)
        lnb = gmlp_ln_b[l].reshape(1, G_WIDTH)
        ws = gmlp_ws[l].astype(BF16)
        bsx = jnp.repeat(gmlp_bs[l].T, G_CH, axis=1)
        wo = w_out[l].astype(BF16)
        w1 = w_ff1[l].astype(BF16)
        w2 = w_ff2[l].astype(BF16)
        l1g, l1b = ln1_g[l].reshape(1, -1), ln1_b[l].reshape(1, -1)
        l2g, l2b = ln2_g[l].reshape(1, -1), ln2_b[l].reshape(1, -1)
        mod = mod_all[l]

        def gains(tm):
            gq = jnp.broadcast_to((q_norm_g[l] * (HEAD_DIM ** -0.5 * LOG2_E))[:, None], (HEAD_DIM, tm))
            gk = jnp.broadcast_to(k_norm_g[l][:, None], (HEAD_DIM, tm))
            return gq, gk

        gq, gk = gains(tm_p)
        (qT, k, v1T, qmT, km, vmT, so, go, grow, k32, v32) = _inproj(
            xp, mod, 0, w_in_l, gq, gk, None, None, gate_b, lng, lnb, ws, bsx,
            rope=False, emit_kv32=True, tm=tm_p)
        att = _attention(qT, k, v1T, tq=TP, kc=TP, hpb=1)
        zero_state = _pack_state(jnp.zeros((BP, 2, M_HEADS, M_DIM, M_DIM), F32),
                                 jnp.zeros((BP, 2, M_HEADS, M_DIM), F32),
                                 jnp.zeros((BP, 2, M_HEADS), F32))
        hf, hb, s_fin, m_fin = _mlstm(qmT, km, vmT, grow, *zero_state)
        xp = _post(att, hf, hb, so, go, xp, mod, 0, wo, l1g, l1b, w1, w2, l2g, l2b, tm=tm_p, ff_chunk=1024)
        ks_.append(k32.reshape(BP, TP, ATT_KV_HEADS, HEAD_DIM))
        vs_.append(v32.reshape(BP, TP, ATT_KV_HEADS, HEAD_DIM))
        Cs_.append(jnp.swapaxes(s_fin[:, :, :M_DIM, :], -1, -2).reshape(BP, 2, M_HEADS, M_DIM, M_DIM))
        ns_.append(s_fin[:, :, M_DIM, :].reshape(BP, 2, M_HEADS, M_DIM))
        ms_.append(m_fin[:, :, 0, 0].reshape(BP, 2, M_HEADS))

        gq, gk = gains(tm_s)
        (qT, k, v1T, qmT, km, vmT, so, go, grow) = _inproj(
            xs, mod, 1, w_in_l, gq, gk, cos_s, sin_s, gate_b, lng, lnb, ws, bsx,
            rope=True, emit_kv32=False, tm=tm_s)
        k_c = jnp.transpose(cache_attn_k[:, l], (0, 2, 1, 3)).astype(BF16)
        ones_col = jnp.zeros(k_c.shape[:-1] + (LANES - HEAD_DIM,), BF16).at[..., 0].set(1.0)
        k_c = jnp.concatenate([k_c, ones_col], axis=-1)
        vT_c = _with_ones_rows(jnp.transpose(cache_attn_v[:, l], (0, 2, 3, 1)).astype(BF16))
        k_all = jnp.concatenate([k_c, k], axis=2)
        v1T_all = jnp.concatenate([vT_c, v1T], axis=3)
        att = _attention(qT, k_all, v1T_all, tq=ATT_TQ, kc=ATT_KC, hpb=ATT_HPB)
        state = _pack_state(state_mlstm_C[:, l], state_mlstm_n[:, l], state_mlstm_m[:, l])
        hf, hb, _, _ = _mlstm(qmT, km, vmT, grow, *state)
        xs = _post(att, hf, hb, so, go, xs, mod, 1, wo, l1g, l1b, w1, w2, l2g, l2b, tm=tm_s, ff_chunk=1024)

    return (xp, xs, jnp.stack(ks_, axis=1), jnp.stack(vs_, axis=1), jnp.stack(Cs_, axis=1),
            jnp.stack(ns_, axis=1), jnp.stack(ms_, axis=1))
```

```python
import functools

import jax
import jax.numpy as jnp
from jax import lax
from jax.experimental import pallas as pl
from jax.experimental.pallas import tpu as pltpu

D_MODEL = 1024
DEPTH = 2
GRID_W = 64
HEAD_DIM = 64
ATT_HEADS = 8
ATT_KV_HEADS = 2
ATT_GROUP = ATT_HEADS // ATT_KV_HEADS
ATT_WIDTH = ATT_HEADS * HEAD_DIM
KV_WIDTH = ATT_KV_HEADS * HEAD_DIM
M_HEADS = 4
M_DIM = 64
M_WIDTH = M_HEADS * M_DIM
G_GROUPS = 4
G_WIDTH = D_MODEL - ATT_WIDTH - M_WIDTH
G_CH = G_WIDTH // G_GROUPS
CHUNK = 128
D_FF = 4 * D_MODEL
ROPE_BASE = 10000.0
EPS = 1e-6
DEEPNORM_ALPHA = (2 * DEPTH) ** 0.25
N_GATES = 4 * M_HEADS

LANES = 128
BF16_SUBLANES = 16
GATE_PAD = LANES
OFF_QA = 0
OFF_KA = OFF_QA + ATT_WIDTH
OFF_VA = OFF_KA + KV_WIDTH
OFF_QM = OFF_VA + KV_WIDTH
OFF_KM = OFF_QM + M_WIDTH
OFF_VM = OFF_KM + M_WIDTH
OFF_OM = OFF_VM + M_WIDTH
OFF_UG = OFF_OM + M_WIDTH
OFF_VG = OFF_UG + G_WIDTH
OFF_GM = OFF_VG + G_WIDTH
PROJ_PAD = OFF_GM + GATE_PAD
V_ROWS = HEAD_DIM + BF16_SUBLANES
VMEM_LIMIT = 56 * 1024 * 1024

ATT_TQ, ATT_KC, ATT_HPB = 512, 512, 2
MLSTM_BATCH = 4
LOG2_E = 1.4426950408889634
BOUND_MARGIN = 1.0 + 2.0 ** -6
MAX_FAST_SHIFT = 60.0

F32 = jnp.float32
BF16 = jnp.bfloat16


def _dot(a, b):
    return jnp.dot(a, b, preferred_element_type=F32)


def _resident(shape):
    nd = len(shape)
    return pl.BlockSpec(shape, lambda *_: (0,) * nd, pipeline_mode=pl.Buffered(1))


def _sigmoid(x):
    return 1.0 / (1.0 + jnp.exp(-x))


def _log_sigmoid(x):
    return jnp.minimum(x, 0.0) - jnp.log1p(jnp.exp(-jnp.abs(x)))


def _layer_norm(x, g, b):
    mu = jnp.mean(x, axis=-1, keepdims=True)
    xc = x - mu
    var = jnp.mean(xc * xc, axis=-1, keepdims=True)
    return xc * lax.rsqrt(var + EPS) * g + b


def _mod_kernel(c_ref, w_ref, b_ref, o_ref):
    c = c_ref[...]
    s = (c * _sigmoid(c)).astype(BF16)
    o_ref[0] = _dot(s, w_ref[0].astype(BF16)) + b_ref[0]


def _modulation(cvec, w_mod, b_mod):
    n = cvec.shape[0]
    bn = 1536
    return pl.pallas_call(
        _mod_kernel,
        grid=(DEPTH, 6 * D_MODEL // bn),
        in_specs=[
            pl.BlockSpec((n, D_MODEL), lambda l, j: (0, 0)),
            pl.BlockSpec((1, D_MODEL, bn), lambda l, j: (l, 0, j)),
            pl.BlockSpec((1, 1, bn), lambda l, j: (l, 0, j)),
        ],
        out_specs=pl.BlockSpec((1, n, bn), lambda l, j: (l, 0, j)),
        out_shape=jax.ShapeDtypeStruct((DEPTH, n, 6 * D_MODEL), F32),
        compiler_params=pltpu.CompilerParams(
            dimension_semantics=("arbitrary", "arbitrary"), vmem_limit_bytes=VMEM_LIMIT),
        name="modulation",
    )(cvec, w_mod, b_mod.reshape(DEPTH, 1, 6 * D_MODEL))


def _norm_rope_heads(zT, gain, cos, sin, n_heads):
    outs = []
    for hd in range(n_heads):
        t = zT[hd * HEAD_DIM:(hd + 1) * HEAD_DIM, :]
        ms = jnp.mean(t * t, axis=0, keepdims=True)
        y = t * lax.rsqrt(ms + EPS) * gain
        if cos is not None:
            q = HEAD_DIM // 4
            sw = jnp.concatenate([y[q:2 * q], y[0:q], y[3 * q:4 * q], y[2 * q:3 * q]], axis=0)
            y = y * cos + sw * sin
        outs.append(y)
    return outs


def _inproj_kernel(*refs, rope, emit_kv32, tm):
    it = iter(refs)
    x_ref, mod_ref, w_ref, gq_ref, gk_ref = (next(it) for _ in range(5))
    cos_ref, sin_ref = (next(it), next(it)) if rope else (None, None)
    gb_ref, lng_ref, lnb_ref, ws_ref, bsx_ref = (next(it) for _ in range(5))
    qT_ref, k_ref, vT_ref, qmT_ref, km_ref, vmT_ref, so_ref, go_ref, grow_ref = (
        next(it) for _ in range(9))
    k32_ref, v32_ref = (next(it), next(it)) if emit_kv32 else (None, None)

    x = x_ref[0]
    mod = mod_ref[0]
    sh1 = mod[:, 0:D_MODEL]
    sc1 = mod[:, D_MODEL:2 * D_MODEL]
    h = (x * (1.0 + sc1) + sh1).astype(BF16)

    cos = cos_ref[...] if rope else None
    sin = sin_ref[...] if rope else None

    zq = _dot(h, w_ref[:, OFF_QA:OFF_QA + ATT_WIDTH])
    q_heads = _norm_rope_heads(zq.T, gq_ref[...], cos, sin, ATT_HEADS)
    for hd in range(ATT_HEADS):
        qT_ref[0, hd] = q_heads[hd].astype(BF16)

    zk = _dot(h, w_ref[:, OFF_KA:OFF_KA + KV_WIDTH])
    k_heads = _norm_rope_heads(zk.T, gk_ref[...], cos, sin, ATT_KV_HEADS)
    kk = jnp.concatenate(k_heads, axis=0).T
    ones_col = (lax.broadcasted_iota(jnp.int32, (tm, LANES - HEAD_DIM), 1) == 0).astype(F32)
    for hd in range(ATT_KV_HEADS):
        k_ref[0, hd] = jnp.concatenate(
            [kk[:, hd * HEAD_DIM:(hd + 1) * HEAD_DIM], ones_col], axis=1).astype(BF16)
    zv = _dot(h, w_ref[:, OFF_VA:OFF_VA + KV_WIDTH])
    vT = zv.T
    ones_rows = jnp.ones((BF16_SUBLANES, tm), BF16)
    for hd in range(ATT_KV_HEADS):
        vT_ref[0, hd] = jnp.concatenate(
            [vT[hd * HEAD_DIM:(hd + 1) * HEAD_DIM].astype(BF16), ones_rows], axis=0)
    if emit_kv32:
        k32_ref[0] = kk
        v32_ref[0] = zv

    zqmT = _dot(h, w_ref[:, OFF_QM:OFF_QM + M_WIDTH]).T
    zkm = _dot(h, w_ref[:, OFF_KM:OFF_KM + M_WIDTH]) * (M_DIM ** -0.5)
    zvmT = _dot(h, w_ref[:, OFF_VM:OFF_VM + M_WIDTH]).T
    for hd in range(M_HEADS):
        qmT_ref[0, hd] = zqmT[hd * M_DIM:(hd + 1) * M_DIM, :].astype(BF16)
        km_ref[0, hd] = zkm[:, hd * M_DIM:(hd + 1) * M_DIM].astype(BF16)
        vmT_ref[0, hd] = zvmT[hd * M_DIM:(hd + 1) * M_DIM, :].astype(BF16)
    zo = _dot(h, w_ref[:, OFF_OM:OFF_OM + M_WIDTH])
    so_ref[0] = _sigmoid(zo).astype(BF16)

    zg = _dot(h, w_ref[:, OFF_GM:OFF_GM + GATE_PAD]) + gb_ref[...]
    lane = lax.broadcasted_iota(jnp.int32, zg.shape, 1)
    is_forget = (lane % (2 * M_HEADS)) >= M_HEADS
    grow_ref[0] = jnp.where(is_forget, _log_sigmoid(zg), zg).T[0:N_GATES, :]

    zu = _dot(h, w_ref[:, OFF_UG:OFF_UG + G_WIDTH])
    zvg = _dot(h, w_ref[:, OFF_VG:OFF_VG + G_WIDTH])
    vn = _layer_norm(zvg, lng_ref[...], lnb_ref[...])
    group = lax.broadcasted_iota(jnp.int32, (CHUNK, G_WIDTH), 1) // G_CH
    bsx = bsx_ref[...]
    s_chunks = []
    for j in range(tm // CHUNK):
        vc = vn[j * CHUNK:(j + 1) * CHUNK, :]
        acc = bsx
        for g in range(G_GROUPS):
            acc = acc + _dot(ws_ref[g], jnp.where(group == g, vc, 0.0).astype(BF16))
        s_chunks.append(acc)
    s = jnp.concatenate(s_chunks, axis=0) if len(s_chunks) > 1 else s_chunks[0]
    go_ref[0] = (zu * s).astype(BF16)


def _inproj(x, mod, mod_row, w_in, gq, gk, cos, sin, gate_b, lng, lnb, ws, bsx, *, rope, emit_kv32, tm):
    B, T, _ = x.shape
    nt = T // tm
    mod_idx = (lambda b, i: (b + mod_row, 0, 0)) if mod_row else (lambda b, i: (0, 0, 0))
    in_specs = [
        pl.BlockSpec((1, tm, D_MODEL), lambda b, i: (b, i, 0)),
        pl.BlockSpec((1, 1, 6 * D_MODEL), mod_idx),
        _resident((D_MODEL, PROJ_PAD)),
        _resident((HEAD_DIM, tm)),
        _resident((HEAD_DIM, tm)),
    ]
    args = [x, mod, w_in, gq, gk]
    if rope:
        in_specs += [pl.BlockSpec((HEAD_DIM, tm), lambda b, i: (0, i))] * 2
        args += [cos, sin]
    in_specs += [
        _resident((1, GATE_PAD)),
        _resident((1, G_WIDTH)),
        _resident((1, G_WIDTH)),
        _resident((G_GROUPS, CHUNK, CHUNK)),
        _resident((CHUNK, G_WIDTH)),
    ]
    args += [gate_b, lng, lnb, ws, bsx]
    out_shape = [
        jax.ShapeDtypeStruct((B, ATT_HEADS, HEAD_DIM, T), BF16),
        jax.ShapeDtypeStruct((B, ATT_KV_HEADS, T, LANES), BF16),
        jax.ShapeDtypeStruct((B, ATT_KV_HEADS, V_ROWS, T), BF16),
        jax.ShapeDtypeStruct((B, M_HEADS, M_DIM, T), BF16),
        jax.ShapeDtypeStruct((B, M_HEADS, T, M_DIM), BF16),
        jax.ShapeDtypeStruct((B, M_HEADS, M_DIM, T), BF16),
        jax.ShapeDtypeStruct((B, T, M_WIDTH), BF16),
        jax.ShapeDtypeStruct((B, T, G_WIDTH), BF16),
        jax.ShapeDtypeStruct((B, N_GATES, T), F32),
    ]
    out_specs = [
        pl.BlockSpec((1, ATT_HEADS, HEAD_DIM, tm), lambda b, i: (b, 0, 0, i)),
        pl.BlockSpec((1, ATT_KV_HEADS, tm, LANES), lambda b, i: (b, 0, i, 0)),
        pl.BlockSpec((1, ATT_KV_HEADS, V_ROWS, tm), lambda b, i: (b, 0, 0, i)),
        pl.BlockSpec((1, M_HEADS, M_DIM, tm), lambda b, i: (b, 0, 0, i)),
        pl.BlockSpec((1, M_HEADS, tm, M_DIM), lambda b, i: (b, 0, i, 0)),
        pl.BlockSpec((1, M_HEADS, M_DIM, tm), lambda b, i: (b, 0, 0, i)),
        pl.BlockSpec((1, tm, M_WIDTH), lambda b, i: (b, i, 0)),
        pl.BlockSpec((1, tm, G_WIDTH), lambda b, i: (b, i, 0)),
        pl.BlockSpec((1, N_GATES, tm), lambda b, i: (b, 0, i)),
    ]
    if emit_kv32:
        out_shape += [jax.ShapeDtypeStruct((B, T, KV_WIDTH), F32)] * 2
        out_specs += [pl.BlockSpec((1, tm, KV_WIDTH), lambda b, i: (b, i, 0))] * 2
    return pl.pallas_call(
        functools.partial(_inproj_kernel, rope=rope, emit_kv32=emit_kv32, tm=tm),
        grid=(B, nt),
        in_specs=in_specs,
        out_specs=out_specs,
        out_shape=out_shape,
        compiler_params=pltpu.CompilerParams(
            dimension_semantics=("parallel", "parallel"), vmem_limit_bytes=VMEM_LIMIT),
        name="inproj",
    )(*args)


def _attn_kernel(qT_ref, k1_ref, v1T_ref, o_ref, oT_scr, q1_scr, kmax_scr, acc_scr, m_scr,
                 *, n_keys, kc, tq, hpb):
    nk = n_keys // kc
    i = pl.program_id(2)

    @pl.when(i == 0)
    def _():
        kf = k1_ref[0, 0].astype(F32)
        n2 = jnp.sum(kf * kf, axis=1, keepdims=True) - 1.0
        kmax_scr[...] = jnp.broadcast_to(jnp.max(n2, axis=0, keepdims=True), kmax_scr.shape)

    kmax2 = kmax_scr[0:1, 0:1]
    row0 = lax.broadcasted_iota(jnp.int32, (BF16_SUBLANES, tq), 0) == 0
    pad = jnp.zeros((LANES - HEAD_DIM - BF16_SUBLANES, tq), BF16)
    cmax = None
    for g in range(ATT_GROUP):
        qT = qT_ref[0, g]
        qf = qT.astype(F32)
        c = jnp.sqrt(jnp.sum(qf * qf, axis=0, keepdims=True) * kmax2) * BOUND_MARGIN
        shift = jnp.where(row0, -c, 0.0).astype(BF16)
        q1_scr[g] = jnp.concatenate([qT, shift, pad], axis=0)
        cg = jnp.max(c)
        cmax = cg if cmax is None else jnp.maximum(cmax, cg)
    fast = cmax <= MAX_FAST_SHIFT

    @pl.when(fast)
    def _():
        def heads(gi, carry):
            gs = [gi * hpb + a for a in range(hpb)]
            q1s = [q1_scr[g] for g in gs]

            def probs(j):
                kblk = k1_ref[0, 0, j * kc:(j + 1) * kc, :]
                return [jnp.exp2(_dot(kblk, q1s[a])).astype(BF16) for a in range(hpb)]

            accs = [None] * hpb
            p_next = probs(0)
            for j in range(nk):
                p_cur = p_next
                if j + 1 < nk:
                    p_next = probs(j + 1)
                vblk = v1T_ref[0, 0, :, j * kc:(j + 1) * kc]
                for a in range(hpb):
                    pv = _dot(vblk, p_cur[a])
                    accs[a] = pv if j == 0 else accs[a] + pv
            for a in range(hpb):
                oT_scr[gs[a]] = accs[a][0:HEAD_DIM] / accs[a][HEAD_DIM:HEAD_DIM + 1]
            return carry

        lax.fori_loop(0, ATT_GROUP // hpb, heads, 0)

    @pl.when(jnp.logical_not(fast))
    def _():
        def head(g, carry):
            qT = qT_ref[0, g]
            m_scr[...] = jnp.full(m_scr.shape, -jnp.inf, F32)
            acc_scr[...] = jnp.zeros(acc_scr.shape, F32)

            def chunk(j, c2):
                ks = pl.multiple_of(j * kc, kc)
                sT = _dot(k1_ref[0, 0, pl.ds(ks, kc), 0:HEAD_DIM], qT)
                m_old = m_scr[...]
                m_new = jnp.maximum(m_old, jnp.max(sT, axis=0, keepdims=True))
                p = jnp.exp2(sT - m_new).astype(BF16)
                acc_scr[...] = jnp.exp2(m_old - m_new) * acc_scr[...] + _dot(v1T_ref[0, 0, :, pl.ds(ks, kc)], p)
                m_scr[...] = m_new
                return c2

            lax.fori_loop(0, nk, chunk, 0)
            acc = acc_scr[...]
            oT_scr[g] = acc[0:HEAD_DIM] / acc[HEAD_DIM:HEAD_DIM + 1]
            return carry

        lax.fori_loop(0, ATT_GROUP, head, 0)

    o_ref[0] = oT_scr[...].reshape(ATT_GROUP * HEAD_DIM, tq).T.astype(BF16)


def _attention(qT, k1, v1T, *, tq, kc, hpb):
    B, _, _, T = qT.shape
    n_keys = k1.shape[2]
    return pl.pallas_call(
        functools.partial(_attn_kernel, n_keys=n_keys, kc=kc, tq=tq, hpb=hpb),
        grid=(B, ATT_KV_HEADS, T // tq),
        in_specs=[
            pl.BlockSpec((1, ATT_GROUP, HEAD_DIM, tq), lambda b, h, i: (b, h, 0, i)),
            pl.BlockSpec((1, 1, n_keys, LANES), lambda b, h, i: (b, h, 0, 0)),
            pl.BlockSpec((1, 1, V_ROWS, n_keys), lambda b, h, i: (b, h, 0, 0)),
        ],
        out_specs=pl.BlockSpec((1, tq, ATT_GROUP * HEAD_DIM), lambda b, h, i: (b, i, h)),
        out_shape=jax.ShapeDtypeStruct((B, T, ATT_WIDTH), BF16),
        scratch_shapes=[
            pltpu.VMEM((ATT_GROUP, HEAD_DIM, tq), F32),
            pltpu.VMEM((ATT_GROUP, LANES, tq), BF16),
            pltpu.VMEM((8, LANES), F32),
            pltpu.VMEM((V_ROWS, tq), F32),
            pltpu.VMEM((1, tq), F32),
        ],
        compiler_params=pltpu.CompilerParams(
            dimension_semantics=("parallel", "parallel", "arbitrary"), vmem_limit_bytes=VMEM_LIMIT),
        name="attention",
    )(qT, k1, v1T)


def _mlstm_kernel(qTf_ref, kf_ref, vTf_ref, qTb_ref, kb_ref, vTb_ref, grf_ref, grb_ref,
                  s0_ref, m0_ref, hf_ref, hb_ref, sout_ref, mout_ref, s_scr, m_scr, *, nc, bb):
    c = pl.program_id(1)
    n_state = 2 * M_HEADS

    @pl.when(c == 0)
    def _():
        for bi in range(bb):
            s_scr[bi * n_state:(bi + 1) * n_state] = s0_ref[bi]
            m_scr[bi * n_state:(bi + 1) * n_state] = m0_ref[bi]

    si = lax.broadcasted_iota(jnp.int32, (CHUNK, CHUNK), 0)
    ti = lax.broadcasted_iota(jnp.int32, (CHUNK, CHUNK), 1)
    ones_v = jnp.ones((M_DIM, CHUNK), BF16)
    zrows = jnp.zeros((CHUNK - M_HEADS, CHUNK), F32)

    dirs = [(bi, d) for bi in range(bb) for d in range(2)]
    valids = [(si <= ti), (si >= ti)]
    gates = []
    for bi, d in dirs:
        gr = (grf_ref if d == 0 else grb_ref)[bi]
        tri = valids[d].astype(BF16)
        gr_hi = gr.astype(BF16)
        gr_lo = (gr - gr_hi.astype(F32)).astype(BF16)
        gates.append((gr, _dot(gr_hi, tri) + _dot(gr_lo, tri)))

    pre = []
    for bi, d in dirs:
        qT_ref, k_ref, vT_ref = (qTf_ref, kf_ref, vTf_ref) if d == 0 else (qTb_ref, kb_ref, vTb_ref)
        for hd in range(M_HEADS):
            qT = qT_ref[bi, hd]
            kh = k_ref[bi, hd]
            v1T = jnp.concatenate([vT_ref[bi, hd], ones_v], axis=0)
            pre.append(dict(qT=qT, kh=kh, v1T=v1T, kq=_dot(kh, qT)))

    for (bi, d), (gr, arow_all) in zip(dirs, gates):
        g0 = 2 * M_HEADS * d
        a_rows = arow_all[g0 + M_HEADS:g0 + 2 * M_HEADS, :]
        r_rows = gr[g0:g0 + M_HEADS, :] - a_rows
        r_cols = jnp.concatenate([r_rows, zrows], axis=0).T
        last = CHUNK - 1 if d == 0 else 0
        for hd in range(M_HEADS):
            e = pre[(2 * bi + d) * M_HEADS + hd]
            a_row = a_rows[hd:hd + 1, :]
            x = jnp.where(valids[d], r_cols[:, hd:hd + 1], -jnp.inf)
            R_row = jnp.max(x, axis=0, keepdims=True)
            R_L = jnp.broadcast_to(R_row[:, last:last + 1], (1, LANES))
            wk = jnp.exp(r_rows[hd:hd + 1, :] - R_L)
            e.update(a_row=a_row, R_row=R_row, R_L=R_L, a_L=a_row[:, last:last + 1],
                     kqd=e["kq"] * jnp.exp(x - R_row),
                     uT=_dot((e["v1T"].astype(F32) * wk).astype(BF16), e["kh"]))

    outs = []
    for j, e in enumerate(pre):
        m = m_scr[j]
        st = s_scr[j]
        M_row = jnp.maximum(m, e["R_row"])
        s0T = (e["kqd"] * jnp.exp(e["R_row"] - M_row)).astype(BF16)
        qs = (e["qT"].astype(F32) * jnp.exp(m - M_row)).astype(BF16)
        lhs = jnp.concatenate([e["v1T"], st.astype(BF16)], axis=1)
        tot = _dot(lhs, jnp.concatenate([s0T, qs], axis=0))
        den = jnp.maximum(jnp.abs(tot[M_DIM:2 * M_DIM]), jnp.exp(-(e["a_row"] + M_row)))
        outs.append(tot[0:M_DIM] / den)
        M_L = jnp.maximum(m, e["R_L"])
        s_scr[j] = jnp.exp(m - M_L)[:, 0:M_DIM] * st + jnp.exp(e["R_L"] - M_L)[:, 0:M_DIM] * e["uT"]
        m_scr[j] = jnp.broadcast_to(e["a_L"], (1, LANES)) + M_L
    for bi in range(bb):
        o = bi * n_state
        hf_ref[bi] = jnp.concatenate(outs[o:o + M_HEADS], axis=0).T
        hb_ref[bi] = jnp.concatenate(outs[o + M_HEADS:o + n_state], axis=0).T

    @pl.when(c == nc - 1)
    def _():
        for bi in range(bb):
            sout_ref[bi] = s_scr[bi * n_state:(bi + 1) * n_state]
            mout_ref[bi] = m_scr[bi * n_state:(bi + 1) * n_state]


def _mlstm(qmT, km, vmT, grow, s0, m0):
    B, _, _, T = qmT.shape
    nc = T // CHUNK
    fwd3 = lambda b, c: (b, 0, c, 0)
    bwd3 = lambda b, c: (b, 0, nc - 1 - c, 0)
    fwdT = lambda b, c: (b, 0, 0, c)
    bwdT = lambda b, c: (b, 0, 0, nc - 1 - c)
    bb = MLSTM_BATCH
    rspec = lambda im: pl.BlockSpec((bb, M_HEADS, CHUNK, M_DIM), im)
    tspec = lambda im: pl.BlockSpec((bb, M_HEADS, M_DIM, CHUNK), im)
    n_state = 2 * M_HEADS
    return pl.pallas_call(
        functools.partial(_mlstm_kernel, nc=nc, bb=bb),
        grid=(B // bb, nc),
        in_specs=[
            tspec(fwdT), rspec(fwd3), tspec(fwdT),
            tspec(bwdT), rspec(bwd3), tspec(bwdT),
            pl.BlockSpec((bb, N_GATES, CHUNK), lambda b, c: (b, 0, c)),
            pl.BlockSpec((bb, N_GATES, CHUNK), lambda b, c: (b, 0, nc - 1 - c)),
            pl.BlockSpec((bb, n_state, 2 * M_DIM, M_DIM), lambda b, c: (b, 0, 0, 0)),
            pl.BlockSpec((bb, n_state, 1, LANES), lambda b, c: (b, 0, 0, 0)),
        ],
        out_specs=[
            pl.BlockSpec((bb, CHUNK, M_WIDTH), lambda b, c: (b, c, 0)),
            pl.BlockSpec((bb, CHUNK, M_WIDTH), lambda b, c: (b, nc - 1 - c, 0)),
            pl.BlockSpec((bb, n_state, 2 * M_DIM, M_DIM), lambda b, c: (b, 0, 0, 0)),
            pl.BlockSpec((bb, n_state, 1, LANES), lambda b, c: (b, 0, 0, 0)),
        ],
        out_shape=[
            jax.ShapeDtypeStruct((B, T, M_WIDTH), F32),
            jax.ShapeDtypeStruct((B, T, M_WIDTH), F32),
            jax.ShapeDtypeStruct((B, n_state, 2 * M_DIM, M_DIM), F32),
            jax.ShapeDtypeStruct((B, n_state, 1, LANES), F32),
        ],
        scratch_shapes=[pltpu.VMEM((bb * n_state, 2 * M_DIM, M_DIM), F32),
                        pltpu.VMEM((bb * n_state, 1, LANES), F32)],
        compiler_params=pltpu.CompilerParams(
            dimension_semantics=("parallel", "arbitrary"), vmem_limit_bytes=VMEM_LIMIT),
        name="mlstm",
    )(qmT, km, vmT, qmT, km, vmT, grow, grow, s0, m0)


def _post_kernel(att_ref, hf_ref, hb_ref, so_ref, go_ref, x_ref, mod_ref, wo_ref, l1g_ref, l1b_ref,
                 w1_ref, w2_ref, l2g_ref, l2b_ref, o_ref, *, ff_chunk):
    mod = mod_ref[0]
    g1 = mod[:, 2 * D_MODEL:3 * D_MODEL]
    sh2 = mod[:, 3 * D_MODEL:4 * D_MODEL]
    sc2 = mod[:, 4 * D_MODEL:5 * D_MODEL]
    g2 = mod[:, 5 * D_MODEL:6 * D_MODEL]
    m_out = (so_ref[0].astype(F32) * (hf_ref[0] + hb_ref[0])).astype(BF16)
    cat = jnp.concatenate([att_ref[0], m_out, go_ref[0]], axis=1)
    y = _dot(cat, wo_ref[...])
    x1 = _layer_norm(DEEPNORM_ALPHA * x_ref[0] + g1 * y, l1g_ref[...], l1b_ref[...])
    h2 = (x1 * (1.0 + sc2) + sh2).astype(BF16)
    f = None
    for j in range(D_FF // ff_chunk):
        a = jnp.maximum(_dot(h2, w1_ref[:, j * ff_chunk:(j + 1) * ff_chunk]), 0.0)
        part = _dot((a * a).astype(BF16), w2_ref[j * ff_chunk:(j + 1) * ff_chunk, :])
        f = part if f is None else f + part
    o_ref[0] = _layer_norm(DEEPNORM_ALPHA * x1 + g2 * f, l2g_ref[...], l2b_ref[...])


def _post(att, hf, hb, so, go, x, mod, mod_row, wo, l1g, l1b, w1, w2, l2g, l2b, *, tm, ff_chunk):
    B, T, _ = x.shape
    mod_idx = (lambda b, i: (b + mod_row, 0, 0)) if mod_row else (lambda b, i: (0, 0, 0))
    tok = lambda w: pl.BlockSpec((1, tm, w), lambda b, i: (b, i, 0))
    return pl.pallas_call(
        functools.partial(_post_kernel, ff_chunk=ff_chunk),
        grid=(B, T // tm),
        in_specs=[
            tok(ATT_WIDTH), tok(M_WIDTH), tok(M_WIDTH), tok(M_WIDTH), tok(G_WIDTH), tok(D_MODEL),
            pl.BlockSpec((1, 1, 6 * D_MODEL), mod_idx),
            _resident((D_MODEL, D_MODEL)), _resident((1, D_MODEL)), _resident((1, D_MODEL)),
            _resident((D_MODEL, D_FF)), _resident((D_FF, D_MODEL)),
            _resident((1, D_MODEL)), _resident((1, D_MODEL)),
        ],
        out_specs=tok(D_MODEL),
        out_shape=jax.ShapeDtypeStruct((B, T, D_MODEL), F32),
        compiler_params=pltpu.CompilerParams(
            dimension_semantics=("parallel", "parallel"), vmem_limit_bytes=VMEM_LIMIT),
        name="post",
    )(att, hf, hb, so, go, x, mod, wo, l1g, l1b, w1, w2, l2g, l2b)


def _rope_tables(T):
    rows = T // GRID_W
    r, cidx = jnp.meshgrid(jnp.arange(rows, dtype=F32), jnp.arange(GRID_W, dtype=F32), indexing='ij')
    n_freq = HEAD_DIM // 4
    inv = ROPE_BASE ** (-jnp.arange(n_freq, dtype=F32) / n_freq)
    ang_r = (r.reshape(-1)[:, None] * inv[None, :]).T
    ang_c = (cidx.reshape(-1)[:, None] * inv[None, :]).T
    cos = jnp.concatenate([jnp.cos(ang_r)] * 2 + [jnp.cos(ang_c)] * 2, axis=0)
    sin = jnp.concatenate([-jnp.sin(ang_r), jnp.sin(ang_r), -jnp.sin(ang_c), jnp.sin(ang_c)], axis=0)
    return cos, sin


def _with_ones_rows(vT):
    ones = jnp.ones(vT.shape[:-2] + (BF16_SUBLANES, vT.shape[-1]), vT.dtype)
    return jnp.concatenate([vT, ones], axis=-2)


def _pack_state(C, n, m):
    B = C.shape[0]
    ns = 2 * M_HEADS
    CT = jnp.swapaxes(C.reshape(B, ns, M_DIM, M_DIM), -1, -2).astype(F32)
    n = jnp.broadcast_to(n.reshape(B, ns, 1, M_DIM).astype(F32), (B, ns, M_DIM, M_DIM))
    m = jnp.broadcast_to(m.reshape(B, ns, 1, 1).astype(F32), (B, ns, 1, LANES))
    return jnp.concatenate([CT, n], axis=-2), m


def kernel(x_prompt, x_sample, c, cache_attn_k, cache_attn_v, state_mlstm_C, state_mlstm_n, state_mlstm_m,
           c_ctx, w_mod, b_mod, w_in, q_norm_g, k_norm_g, mlstm_gate_b, gmlp_ln_g, gmlp_ln_b, gmlp_ws,
           gmlp_bs, w_out, ln1_g, ln1_b, w_ff1, w_ff2, ln2_g, ln2_b):
    BP, TP, _ = x_prompt.shape
    BS, TS, _ = x_sample.shape
    n_mod = 16
    cvec = jnp.zeros((n_mod, D_MODEL), F32).at[0].set(c_ctx).at[1:1 + BS].set(c)
    mod_all = _modulation(cvec, w_mod, b_mod).reshape(DEPTH, n_mod, 1, 6 * D_MODEL)

    cos_s, sin_s = _rope_tables(TS)
    tm_p, tm_s = TP, 512

    xp, xs = x_prompt, x_sample
    ks_, vs_, Cs_, ns_, ms_ = [], [], [], [], []
    for l in range(DEPTH):
        wl = w_in[l]
        g0 = OFF_OM + M_WIDTH
        w_in_l = jnp.concatenate(
            [wl[:, :g0], wl[:, g0 + N_GATES:], wl[:, g0:g0 + N_GATES],
             jnp.zeros((D_MODEL, GATE_PAD - N_GATES), F32)], axis=1).astype(BF16)
        gate_b = jnp.zeros((1, GATE_PAD), F32).at[0, :N_GATES].set(mlstm_gate_b[l].reshape(-1))
        lng = gmlp_ln_g[l].reshape(1, G_WIDTH)
        lnb = gmlp_ln_b[l].reshape(1, G_WIDTH)
        ws = gmlp_ws[l].astype(BF16)
        bsx = jnp.repeat(gmlp_bs[l].T, G_CH, axis=1)
        wo = w_out[l].astype(BF16)
        w1 = w_ff1[l].astype(BF16)
        w2 = w_ff2[l].astype(BF16)
        l1g, l1b = ln1_g[l].reshape(1, -1), ln1_b[l].reshape(1, -1)
        l2g, l2b = ln2_g[l].reshape(1, -1), ln2_b[l].reshape(1, -1)
        mod = mod_all[l]

        def gains(tm):
            gq = jnp.broadcast_to((q_norm_g[l] * (HEAD_DIM ** -0.5 * LOG2_E))[:, None], (HEAD_DIM, tm))
            gk = jnp.broadcast_to(k_norm_g[l][:, None], (HEAD_DIM, tm))
            return gq, gk

        gq, gk = gains(tm_p)
        (qT, k, v1T, qmT, km, vmT, so, go, grow, k32, v32) = _inproj(
            xp, mod, 0, w_in_l, gq, gk, None, None, gate_b, lng, lnb, ws, bsx,
            rope=False, emit_kv32=True, tm=tm_p)
        att = _attention(qT, k, v1T, tq=TP, kc=TP, hpb=1)
        zero_state = _pack_state(jnp.zeros((BP, 2, M_HEADS, M_DIM, M_DIM), F32),
                                 jnp.zeros((BP, 2, M_HEADS, M_DIM), F32),
                                 jnp.zeros((BP, 2, M_HEADS), F32))
        hf, hb, s_fin, m_fin = _mlstm(qmT, km, vmT, grow, *zero_state)
        xp = _post(att, hf, hb, so, go, xp, mod, 0, wo, l1g, l1b, w1, w2, l2g, l2b, tm=tm_p, ff_chunk=1024)
        ks_.append(k32.reshape(BP, TP, ATT_KV_HEADS, HEAD_DIM))
        vs_.append(v32.reshape(BP, TP, ATT_KV_HEADS, HEAD_DIM))
        Cs_.append(jnp.swapaxes(s_fin[:, :, :M_DIM, :], -1, -2).reshape(BP, 2, M_HEADS, M_DIM, M_DIM))
        ns_.append(s_fin[:, :, M_DIM, :].reshape(BP, 2, M_HEADS, M_DIM))
        ms_.append(m_fin[:, :, 0, 0].reshape(BP, 2, M_HEADS))

        gq, gk = gains(tm_s)
        (qT, k, v1T, qmT, km, vmT, so, go, grow) = _inproj(
            xs, mod, 1, w_in_l, gq, gk, cos_s, sin_s, gate_b, lng, lnb, ws, bsx,
            rope=True, emit_kv32=False, tm=tm_s)
        k_c = jnp.transpose(cache_attn_k[:, l], (0, 2, 1, 3)).astype(BF16)
        ones_col = jnp.zeros(k_c.shape[:-1] + (LANES - HEAD_DIM,), BF16).at[..., 0].set(1.0)
        k_c = jnp.concatenate([k_c, ones_col], axis=-1)
        vT_c = _with_ones_rows(jnp.transpose(cache_attn_v[:, l], (0, 2, 3, 1)).astype(BF16))
        k_all = jnp.concatenate([k_c, k], axis=2)
        v1T_all = jnp.concatenate([vT_c, v1T], axis=3)
        att = _attention(qT, k_all, v1T_all, tq=ATT_TQ, kc=ATT_KC, hpb=ATT_HPB)
        state = _pack_state(state_mlstm_C[:, l], state_mlstm_n[:, l], state_mlstm_m[:, l])
        hf, hb, _, _ = _mlstm(qmT, km, vmT, grow, *state)
        xs = _post(att, hf, hb, so, go, xs, mod, 1, wo, l1g, l1b, w1, w2, l2g, l2b, tm=1024, ff_chunk=1024)

    return (xp, xs, jnp.stack(ks_, axis=1), jnp.stack(vs_, axis=1), jnp.stack(Cs_, axis=1),
            jnp.stack(ns_, axis=1), jnp.stack(ms_, axis=1))
```

```python
import functools

import jax
import jax.numpy as jnp
from jax import lax
from jax.experimental import pallas as pl
from jax.experimental.pallas import tpu as pltpu

D_MODEL = 1024
DEPTH = 2
GRID_W = 64
HEAD_DIM = 64
ATT_HEADS = 8
ATT_KV_HEADS = 2
ATT_GROUP = ATT_HEADS // ATT_KV_HEADS
ATT_WIDTH = ATT_HEADS * HEAD_DIM
KV_WIDTH = ATT_KV_HEADS * HEAD_DIM
M_HEADS = 4
M_DIM = 64
M_WIDTH = M_HEADS * M_DIM
G_GROUPS = 4
G_WIDTH = D_MODEL - ATT_WIDTH - M_WIDTH
G_CH = G_WIDTH // G_GROUPS
CHUNK = 128
D_FF = 4 * D_MODEL
ROPE_BASE = 10000.0
EPS = 1e-6
DEEPNORM_ALPHA = (2 * DEPTH) ** 0.25
N_GATES = 4 * M_HEADS

LANES = 128
BF16_SUBLANES = 16
GATE_PAD = LANES
OFF_QA = 0
OFF_KA = OFF_QA + ATT_WIDTH
OFF_VA = OFF_KA + KV_WIDTH
OFF_QM = OFF_VA + KV_WIDTH
OFF_KM = OFF_QM + M_WIDTH
OFF_VM = OFF_KM + M_WIDTH
OFF_OM = OFF_VM + M_WIDTH
OFF_UG = OFF_OM + M_WIDTH
OFF_VG = OFF_UG + G_WIDTH
OFF_GM = OFF_VG + G_WIDTH
PROJ_PAD = OFF_GM + GATE_PAD
V_ROWS = HEAD_DIM + BF16_SUBLANES
VMEM_LIMIT = 56 * 1024 * 1024

ATT_TQ, ATT_KC, ATT_HPB = 1024, 256, 2
POST_TM = 1024
MLSTM_BATCH = 4
LOG2_E = 1.4426950408889634
BOUND_MARGIN = 1.0 + 2.0 ** -6
MAX_FAST_SHIFT = 60.0

F32 = jnp.float32
BF16 = jnp.bfloat16


def _dot(a, b):
    return jnp.dot(a, b, preferred_element_type=F32)


def _resident(shape):
    nd = len(shape)
    return pl.BlockSpec(shape, lambda *_: (0,) * nd, pipeline_mode=pl.Buffered(1))


def _sigmoid(x):
    return 1.0 / (1.0 + jnp.exp(-x))


def _log_sigmoid(x):
    return jnp.minimum(x, 0.0) - jnp.log1p(jnp.exp(-jnp.abs(x)))


def _layer_norm(x, g, b):
    mu = jnp.mean(x, axis=-1, keepdims=True)
    xc = x - mu
    var = jnp.mean(xc * xc, axis=-1, keepdims=True)
    return xc * lax.rsqrt(var + EPS) * g + b


def _mod_kernel(c_ref, w_ref, b_ref, o_ref):
    c = c_ref[...]
    s = (c * _sigmoid(c)).astype(BF16)
    o_ref[0] = _dot(s, w_ref[0].astype(BF16)) + b_ref[0]


def _modulation(cvec, w_mod, b_mod):
    n = cvec.shape[0]
    bn = 1536
    return pl.pallas_call(
        _mod_kernel,
        grid=(DEPTH, 6 * D_MODEL // bn),
        in_specs=[
            pl.BlockSpec((n, D_MODEL), lambda l, j: (0, 0)),
            pl.BlockSpec((1, D_MODEL, bn), lambda l, j: (l, 0, j)),
            pl.BlockSpec((1, 1, bn), lambda l, j: (l, 0, j)),
        ],
        out_specs=pl.BlockSpec((1, n, bn), lambda l, j: (l, 0, j)),
        out_shape=jax.ShapeDtypeStruct((DEPTH, n, 6 * D_MODEL), F32),
        compiler_params=pltpu.CompilerParams(
            dimension_semantics=("arbitrary", "arbitrary"), vmem_limit_bytes=VMEM_LIMIT),
        name="modulation",
    )(cvec, w_mod, b_mod.reshape(DEPTH, 1, 6 * D_MODEL))


def _norm_rope_heads(zT, gain, cos, sin, n_heads):
    outs = []
    for hd in range(n_heads):
        t = zT[hd * HEAD_DIM:(hd + 1) * HEAD_DIM, :]
        ms = jnp.mean(t * t, axis=0, keepdims=True)
        y = t * lax.rsqrt(ms + EPS) * gain
        if cos is not None:
            q = HEAD_DIM // 4
            sw = jnp.concatenate([y[q:2 * q], y[0:q], y[3 * q:4 * q], y[2 * q:3 * q]], axis=0)
            y = y * cos + sw * sin
        outs.append(y)
    return outs


def _inproj_kernel(*refs, rope, emit_kv32, tm):
    it = iter(refs)
    x_ref, mod_ref, w_ref, gq_ref, gk_ref = (next(it) for _ in range(5))
    cos_ref, sin_ref = (next(it), next(it)) if rope else (None, None)
    gb_ref, lng_ref, lnb_ref, ws_ref, bsx_ref = (next(it) for _ in range(5))
    qT_ref, k_ref, vT_ref, qmT_ref, km_ref, vmT_ref, so_ref, go_ref, grow_ref = (
        next(it) for _ in range(9))
    k32_ref, v32_ref = (next(it), next(it)) if emit_kv32 else (None, None)

    x = x_ref[0]
    mod = mod_ref[0]
    sh1 = mod[:, 0:D_MODEL]
    sc1 = mod[:, D_MODEL:2 * D_MODEL]
    h = (x * (1.0 + sc1) + sh1).astype(BF16)

    cos = cos_ref[...] if rope else None
    sin = sin_ref[...] if rope else None

    zvg = _dot(h, w_ref[:, OFF_VG:OFF_VG + G_WIDTH])
    zu = _dot(h, w_ref[:, OFF_UG:OFF_UG + G_WIDTH])
    vn = _layer_norm(zvg, lng_ref[...], lnb_ref[...])

    zq = _dot(h, w_ref[:, OFF_QA:OFF_QA + ATT_WIDTH])
    q_heads = _norm_rope_heads(zq.T, gq_ref[...], cos, sin, ATT_HEADS)
    for hd in range(ATT_HEADS):
        qT_ref[0, hd] = q_heads[hd].astype(BF16)

    zk = _dot(h, w_ref[:, OFF_KA:OFF_KA + KV_WIDTH])
    k_heads = _norm_rope_heads(zk.T, gk_ref[...], cos, sin, ATT_KV_HEADS)
    kk = jnp.concatenate(k_heads, axis=0).T
    ones_col = (lax.broadcasted_iota(jnp.int32, (tm, LANES - HEAD_DIM), 1) == 0).astype(F32)
    for hd in range(ATT_KV_HEADS):
        k_ref[0, hd] = jnp.concatenate(
            [kk[:, hd * HEAD_DIM:(hd + 1) * HEAD_DIM], ones_col], axis=1).astype(BF16)
    zv = _dot(h, w_ref[:, OFF_VA:OFF_VA + KV_WIDTH])
    vT = zv.T
    ones_rows = jnp.ones((BF16_SUBLANES, tm), BF16)
    for hd in range(ATT_KV_HEADS):
        vT_ref[0, hd] = jnp.concatenate(
            [vT[hd * HEAD_DIM:(hd + 1) * HEAD_DIM].astype(BF16), ones_rows], axis=0)
    if emit_kv32:
        k32_ref[0] = kk
        v32_ref[0] = zv

    zqmT = _dot(h, w_ref[:, OFF_QM:OFF_QM + M_WIDTH]).T
    zkm = _dot(h, w_ref[:, OFF_KM:OFF_KM + M_WIDTH]) * (M_DIM ** -0.5)
    zvmT = _dot(h, w_ref[:, OFF_VM:OFF_VM + M_WIDTH]).T
    for hd in range(M_HEADS):
        qmT_ref[0, hd] = zqmT[hd * M_DIM:(hd + 1) * M_DIM, :].astype(BF16)
        km_ref[0, hd] = zkm[:, hd * M_DIM:(hd + 1) * M_DIM].astype(BF16)
        vmT_ref[0, hd] = zvmT[hd * M_DIM:(hd + 1) * M_DIM, :].astype(BF16)
    zo = _dot(h, w_ref[:, OFF_OM:OFF_OM + M_WIDTH])
    so_ref[0] = _sigmoid(zo).astype(BF16)

    zg = _dot(h, w_ref[:, OFF_GM:OFF_GM + GATE_PAD]) + gb_ref[...]
    lane = lax.broadcasted_iota(jnp.int32, zg.shape, 1)
    is_forget = (lane % (2 * M_HEADS)) >= M_HEADS
    grow_ref[0] = jnp.where(is_forget, _log_sigmoid(zg), zg).T[0:N_GATES, :]

    group = lax.broadcasted_iota(jnp.int32, (CHUNK, G_WIDTH), 1) // G_CH
    bsx = bsx_ref[...]
    s_chunks = []
    for j in range(tm // CHUNK):
        vc = vn[j * CHUNK:(j + 1) * CHUNK, :]
        acc = bsx
        for g in range(G_GROUPS):
            acc = acc + _dot(ws_ref[g], jnp.where(group == g, vc, 0.0).astype(BF16))
        s_chunks.append(acc)
    s = jnp.concatenate(s_chunks, axis=0) if len(s_chunks) > 1 else s_chunks[0]
    go_ref[0] = (zu * s).astype(BF16)


def _inproj(x, mod, mod_row, w_in, gq, gk, cos, sin, gate_b, lng, lnb, ws, bsx, *, rope, emit_kv32, tm):
    B, T, _ = x.shape
    nt = T // tm
    mod_idx = (lambda b, i: (b + mod_row, 0, 0)) if mod_row else (lambda b, i: (0, 0, 0))
    in_specs = [
        pl.BlockSpec((1, tm, D_MODEL), lambda b, i: (b, i, 0)),
        pl.BlockSpec((1, 1, 6 * D_MODEL), mod_idx),
        _resident((D_MODEL, PROJ_PAD)),
        _resident((HEAD_DIM, tm)),
        _resident((HEAD_DIM, tm)),
    ]
    args = [x, mod, w_in, gq, gk]
    if rope:
        in_specs += [pl.BlockSpec((HEAD_DIM, tm), lambda b, i: (0, i))] * 2
        args += [cos, sin]
    in_specs += [
        _resident((1, GATE_PAD)),
        _resident((1, G_WIDTH)),
        _resident((1, G_WIDTH)),
        _resident((G_GROUPS, CHUNK, CHUNK)),
        _resident((CHUNK, G_WIDTH)),
    ]
    args += [gate_b, lng, lnb, ws, bsx]
    out_shape = [
        jax.ShapeDtypeStruct((B, ATT_HEADS, HEAD_DIM, T), BF16),
        jax.ShapeDtypeStruct((B, ATT_KV_HEADS, T, LANES), BF16),
        jax.ShapeDtypeStruct((B, ATT_KV_HEADS, V_ROWS, T), BF16),
        jax.ShapeDtypeStruct((B, M_HEADS, M_DIM, T), BF16),
        jax.ShapeDtypeStruct((B, M_HEADS, T, M_DIM), BF16),
        jax.ShapeDtypeStruct((B, M_HEADS, M_DIM, T), BF16),
        jax.ShapeDtypeStruct((B, T, M_WIDTH), BF16),
        jax.ShapeDtypeStruct((B, T, G_WIDTH), BF16),
        jax.ShapeDtypeStruct((B, N_GATES, T), F32),
    ]
    out_specs = [
        pl.BlockSpec((1, ATT_HEADS, HEAD_DIM, tm), lambda b, i: (b, 0, 0, i)),
        pl.BlockSpec((1, ATT_KV_HEADS, tm, LANES), lambda b, i: (b, 0, i, 0)),
        pl.BlockSpec((1, ATT_KV_HEADS, V_ROWS, tm), lambda b, i: (b, 0, 0, i)),
        pl.BlockSpec((1, M_HEADS, M_DIM, tm), lambda b, i: (b, 0, 0, i)),
        pl.BlockSpec((1, M_HEADS, tm, M_DIM), lambda b, i: (b, 0, i, 0)),
        pl.BlockSpec((1, M_HEADS, M_DIM, tm), lambda b, i: (b, 0, 0, i)),
        pl.BlockSpec((1, tm, M_WIDTH), lambda b, i: (b, i, 0)),
        pl.BlockSpec((1, tm, G_WIDTH), lambda b, i: (b, i, 0)),
        pl.BlockSpec((1, N_GATES, tm), lambda b, i: (b, 0, i)),
    ]
    if emit_kv32:
        out_shape += [jax.ShapeDtypeStruct((B, T, KV_WIDTH), F32)] * 2
        out_specs += [pl.BlockSpec((1, tm, KV_WIDTH), lambda b, i: (b, i, 0))] * 2
    return pl.pallas_call(
        functools.partial(_inproj_kernel, rope=rope, emit_kv32=emit_kv32, tm=tm),
        grid=(B, nt),
        in_specs=in_specs,
        out_specs=out_specs,
        out_shape=out_shape,
        compiler_params=pltpu.CompilerParams(
            dimension_semantics=("parallel", "parallel"), vmem_limit_bytes=VMEM_LIMIT),
        name="inproj",
    )(*args)


def _attn_kernel(qT_ref, k1_ref, v1T_ref, o_ref, oT_scr, q1_scr, kmax_scr, acc_scr, m_scr,
                 *, n_keys, kc, tq, hpb):
    nk = n_keys // kc
    i = pl.program_id(2)

    @pl.when(i == 0)
    def _():
        kf = k1_ref[0, 0].astype(F32)
        n2 = jnp.sum(kf * kf, axis=1, keepdims=True) - 1.0
        kmax_scr[...] = jnp.broadcast_to(jnp.max(n2, axis=0, keepdims=True), kmax_scr.shape)

    kmax2 = kmax_scr[0:1, 0:1]
    row0 = lax.broadcasted_iota(jnp.int32, (BF16_SUBLANES, tq), 0) == 0
    pad = jnp.zeros((LANES - HEAD_DIM - BF16_SUBLANES, tq), BF16)
    cmax = None
    for g in range(ATT_GROUP):
        qT = qT_ref[0, g]
        qf = qT.astype(F32)
        c = jnp.sqrt(jnp.sum(qf * qf, axis=0, keepdims=True) * kmax2) * BOUND_MARGIN
        shift = jnp.where(row0, -c, 0.0).astype(BF16)
        q1_scr[g] = jnp.concatenate([qT, shift, pad], axis=0)
        cg = jnp.max(c)
        cmax = cg if cmax is None else jnp.maximum(cmax, cg)
    fast = cmax <= MAX_FAST_SHIFT

    @pl.when(fast)
    def _():
        def heads(gi, carry):
            gs = [gi * hpb + a for a in range(hpb)]
            q1s = [q1_scr[g] for g in gs]

            def probs(j):
                kblk = k1_ref[0, 0, j * kc:(j + 1) * kc, :]
                return [jnp.exp2(_dot(kblk, q1s[a])).astype(BF16) for a in range(hpb)]

            accs = [None] * hpb
            p_next = probs(0)
            for j in range(nk):
                p_cur = p_next
                if j + 1 < nk:
                    p_next = probs(j + 1)
                vblk = v1T_ref[0, 0, :, j * kc:(j + 1) * kc]
                for a in range(hpb):
                    pv = _dot(vblk, p_cur[a])
                    accs[a] = pv if j == 0 else accs[a] + pv
            for a in range(hpb):
                oT_scr[gs[a]] = accs[a][0:HEAD_DIM] / accs[a][HEAD_DIM:HEAD_DIM + 1]
            return carry

        lax.fori_loop(0, ATT_GROUP // hpb, heads, 0)

    @pl.when(jnp.logical_not(fast))
    def _():
        def head(g, carry):
            qT = qT_ref[0, g]
            m_scr[...] = jnp.full(m_scr.shape, -jnp.inf, F32)
            acc_scr[...] = jnp.zeros(acc_scr.shape, F32)

            def chunk(j, c2):
                ks = pl.multiple_of(j * kc, kc)
                sT = _dot(k1_ref[0, 0, pl.ds(ks, kc), 0:HEAD_DIM], qT)
                m_old = m_scr[...]
                m_new = jnp.maximum(m_old, jnp.max(sT, axis=0, keepdims=True))
                p = jnp.exp2(sT - m_new).astype(BF16)
                acc_scr[...] = jnp.exp2(m_old - m_new) * acc_scr[...] + _dot(v1T_ref[0, 0, :, pl.ds(ks, kc)], p)
                m_scr[...] = m_new
                return c2

            lax.fori_loop(0, nk, chunk, 0)
            acc = acc_scr[...]
            oT_scr[g] = acc[0:HEAD_DIM] / acc[HEAD_DIM:HEAD_DIM + 1]
            return carry

        lax.fori_loop(0, ATT_GROUP, head, 0)

    o_ref[0] = oT_scr[...].reshape(ATT_GROUP * HEAD_DIM, tq).T.astype(BF16)


def _attention(qT, k1, v1T, *, tq, kc, hpb):
    B, _, _, T = qT.shape
    n_keys = k1.shape[2]
    return pl.pallas_call(
        functools.partial(_attn_kernel, n_keys=n_keys, kc=kc, tq=tq, hpb=hpb),
        grid=(B, ATT_KV_HEADS, T // tq),
        in_specs=[
            pl.BlockSpec((1, ATT_GROUP, HEAD_DIM, tq), lambda b, h, i: (b, h, 0, i)),
            pl.BlockSpec((1, 1, n_keys, LANES), lambda b, h, i: (b, h, 0, 0)),
            pl.BlockSpec((1, 1, V_ROWS, n_keys), lambda b, h, i: (b, h, 0, 0)),
        ],
        out_specs=pl.BlockSpec((1, tq, ATT_GROUP * HEAD_DIM), lambda b, h, i: (b, i, h)),
        out_shape=jax.ShapeDtypeStruct((B, T, ATT_WIDTH), BF16),
        scratch_shapes=[
            pltpu.VMEM((ATT_GROUP, HEAD_DIM, tq), F32),
            pltpu.VMEM((ATT_GROUP, LANES, tq), BF16),
            pltpu.VMEM((8, LANES), F32),
            pltpu.VMEM((V_ROWS, tq), F32),
            pltpu.VMEM((1, tq), F32),
        ],
        compiler_params=pltpu.CompilerParams(
            dimension_semantics=("parallel", "parallel", "arbitrary"), vmem_limit_bytes=VMEM_LIMIT),
        name="attention",
    )(qT, k1, v1T)


def _mlstm_kernel(qTf_ref, kf_ref, vTf_ref, qTb_ref, kb_ref, vTb_ref, grf_ref, grb_ref,
                  s0_ref, m0_ref, hf_ref, hb_ref, sout_ref, mout_ref, s_scr, m_scr, *, nc, bb):
    c = pl.program_id(1)
    n_state = 2 * M_HEADS

    @pl.when(c == 0)
    def _():
        for bi in range(bb):
            s_scr[bi * n_state:(bi + 1) * n_state] = s0_ref[bi]
            m_scr[bi * n_state:(bi + 1) * n_state] = m0_ref[bi]

    si = lax.broadcasted_iota(jnp.int32, (CHUNK, CHUNK), 0)
    ti = lax.broadcasted_iota(jnp.int32, (CHUNK, CHUNK), 1)
    ones_v = jnp.ones((M_DIM, CHUNK), BF16)
    zrows = jnp.zeros((CHUNK - M_HEADS, CHUNK), F32)

    dirs = [(bi, d) for bi in range(bb) for d in range(2)]
    valids = [(si <= ti), (si >= ti)]
    gates = []
    for bi, d in dirs:
        gr = (grf_ref if d == 0 else grb_ref)[bi]
        tri = valids[d].astype(BF16)
        gr_hi = gr.astype(BF16)
        gr_lo = (gr - gr_hi.astype(F32)).astype(BF16)
        gates.append((gr, _dot(gr_hi, tri) + _dot(gr_lo, tri)))

    pre = []
    for bi, d in dirs:
        qT_ref, k_ref, vT_ref = (qTf_ref, kf_ref, vTf_ref) if d == 0 else (qTb_ref, kb_ref, vTb_ref)
        for hd in range(M_HEADS):
            qT = qT_ref[bi, hd]
            kh = k_ref[bi, hd]
            v1T = jnp.concatenate([vT_ref[bi, hd], ones_v], axis=0)
            pre.append(dict(qT=qT, kh=kh, v1T=v1T, kq=_dot(kh, qT)))

    for (bi, d), (gr, arow_all) in zip(dirs, gates):
        g0 = 2 * M_HEADS * d
        a_rows = arow_all[g0 + M_HEADS:g0 + 2 * M_HEADS, :]
        r_rows = gr[g0:g0 + M_HEADS, :] - a_rows
        r_cols = jnp.concatenate([r_rows, zrows], axis=0).T
        last = CHUNK - 1 if d == 0 else 0
        for hd in range(M_HEADS):
            e = pre[(2 * bi + d) * M_HEADS + hd]
            a_row = a_rows[hd:hd + 1, :]
            x = jnp.where(valids[d], r_cols[:, hd:hd + 1], -jnp.inf)
            R_row = jnp.max(x, axis=0, keepdims=True)
            R_L = jnp.broadcast_to(R_row[:, last:last + 1], (1, LANES))
            wk = jnp.exp(r_rows[hd:hd + 1, :] - R_L)
            e.update(a_row=a_row, R_row=R_row, R_L=R_L, a_L=a_row[:, last:last + 1],
                     kqd=e["kq"] * jnp.exp(x - R_row),
                     uT=_dot((e["v1T"].astype(F32) * wk).astype(BF16), e["kh"]))

    outs = []
    for j, e in enumerate(pre):
        m = m_scr[j]
        st = s_scr[j]
        M_row = jnp.maximum(m, e["R_row"])
        s0T = (e["kqd"] * jnp.exp(e["R_row"] - M_row)).astype(BF16)
        qs = (e["qT"].astype(F32) * jnp.exp(m - M_row)).astype(BF16)
        lhs = jnp.concatenate([e["v1T"], st.astype(BF16)], axis=1)
        tot = _dot(lhs, jnp.concatenate([s0T, qs], axis=0))
        den = jnp.maximum(jnp.abs(tot[M_DIM:2 * M_DIM]), jnp.exp(-(e["a_row"] + M_row)))
        outs.append(tot[0:M_DIM] / den)
        M_L = jnp.maximum(m, e["R_L"])
        s_scr[j] = jnp.exp(m - M_L)[:, 0:M_DIM] * st + jnp.exp(e["R_L"] - M_L)[:, 0:M_DIM] * e["uT"]
        m_scr[j] = jnp.broadcast_to(e["a_L"], (1, LANES)) + M_L
    for bi in range(bb):
        o = bi * n_state
        hf_ref[bi] = jnp.concatenate(outs[o:o + M_HEADS], axis=0).T
        hb_ref[bi] = jnp.concatenate(outs[o + M_HEADS:o + n_state], axis=0).T

    @pl.when(c == nc - 1)
    def _():
        for bi in range(bb):
            sout_ref[bi] = s_scr[bi * n_state:(bi + 1) * n_state]
            mout_ref[bi] = m_scr[bi * n_state:(bi + 1) * n_state]


def _mlstm(qmT, km, vmT, grow, s0, m0):
    B, _, _, T = qmT.shape
    nc = T // CHUNK
    fwd3 = lambda b, c: (b, 0, c, 0)
    bwd3 = lambda b, c: (b, 0, nc - 1 - c, 0)
    fwdT = lambda b, c: (b, 0, 0, c)
    bwdT = lambda b, c: (b, 0, 0, nc - 1 - c)
    bb = MLSTM_BATCH
    rspec = lambda im: pl.BlockSpec((bb, M_HEADS, CHUNK, M_DIM), im)
    tspec = lambda im: pl.BlockSpec((bb, M_HEADS, M_DIM, CHUNK), im)
    n_state = 2 * M_HEADS
    return pl.pallas_call(
        functools.partial(_mlstm_kernel, nc=nc, bb=bb),
        grid=(B // bb, nc),
        in_specs=[
            tspec(fwdT), rspec(fwd3), tspec(fwdT),
            tspec(bwdT), rspec(bwd3), tspec(bwdT),
            pl.BlockSpec((bb, N_GATES, CHUNK), lambda b, c: (b, 0, c)),
            pl.BlockSpec((bb, N_GATES, CHUNK), lambda b, c: (b, 0, nc - 1 - c)),
            pl.BlockSpec((bb, n_state, 2 * M_DIM, M_DIM), lambda b, c: (b, 0, 0, 0)),
            pl.BlockSpec((bb, n_state, 1, LANES), lambda b, c: (b, 0, 0, 0)),
        ],
        out_specs=[
            pl.BlockSpec((bb, CHUNK, M_WIDTH), lambda b, c: (b, c, 0)),
            pl.BlockSpec((bb, CHUNK, M_WIDTH), lambda b, c: (b, nc - 1 - c, 0)),
            pl.BlockSpec((bb, n_state, 2 * M_DIM, M_DIM), lambda b, c: (b, 0, 0, 0)),
            pl.BlockSpec((bb, n_state, 1, LANES), lambda b, c: (b, 0, 0, 0)),
        ],
        out_shape=[
            jax.ShapeDtypeStruct((B, T, M_WIDTH), F32),
            jax.ShapeDtypeStruct((B, T, M_WIDTH), F32),
            jax.ShapeDtypeStruct((B, n_state, 2 * M_DIM, M_DIM), F32),
            jax.ShapeDtypeStruct((B, n_state, 1, LANES), F32),
        ],
        scratch_shapes=[pltpu.VMEM((bb * n_state, 2 * M_DIM, M_DIM), F32),
                        pltpu.VMEM((bb * n_state, 1, LANES), F32)],
        compiler_params=pltpu.CompilerParams(
            dimension_semantics=("parallel", "arbitrary"), vmem_limit_bytes=VMEM_LIMIT),
        name="mlstm",
    )(qmT, km, vmT, qmT, km, vmT, grow, grow, s0, m0)


def _post_kernel(att_ref, hf_ref, hb_ref, so_ref, go_ref, x_ref, mod_ref, wo_ref, l1g_ref, l1b_ref,
                 w1_ref, w2_ref, l2g_ref, l2b_ref, o_ref, *, ff_chunk):
    mod = mod_ref[0]
    g1 = mod[:, 2 * D_MODEL:3 * D_MODEL]
    sh2 = mod[:, 3 * D_MODEL:4 * D_MODEL]
    sc2 = mod[:, 4 * D_MODEL:5 * D_MODEL]
    g2 = mod[:, 5 * D_MODEL:6 * D_MODEL]
    half = att_ref.shape[1] // 2
    rows = [slice(r * half, (r + 1) * half) for r in range(2)]
    ys = []
    for r in rows:
        m_out = (so_ref[0, r].astype(F32) * (hf_ref[0, r] + hb_ref[0, r])).astype(BF16)
        cat = jnp.concatenate([att_ref[0, r], m_out, go_ref[0, r]], axis=1)
        ys.append(_dot(cat, wo_ref[...]))
    for r, y in zip(rows, ys):
        x1 = _layer_norm(DEEPNORM_ALPHA * x_ref[0, r] + g1 * y, l1g_ref[...], l1b_ref[...])
        h2 = (x1 * (1.0 + sc2) + sh2).astype(BF16)
        f = None
        for j in range(D_FF // ff_chunk):
            a = jnp.maximum(_dot(h2, w1_ref[:, j * ff_chunk:(j + 1) * ff_chunk]), 0.0)
            part = _dot((a * a).astype(BF16), w2_ref[j * ff_chunk:(j + 1) * ff_chunk, :])
            f = part if f is None else f + part
        o_ref[0, r] = _layer_norm(DEEPNORM_ALPHA * x1 + g2 * f, l2g_ref[...], l2b_ref[...])


def _post(att, hf, hb, so, go, x, mod, mod_row, wo, l1g, l1b, w1, w2, l2g, l2b, *, tm, ff_chunk):
    B, T, _ = x.shape
    mod_idx = (lambda b, i: (b + mod_row, 0, 0)) if mod_row else (lambda b, i: (0, 0, 0))
    tok = lambda w: pl.BlockSpec((1, tm, w), lambda b, i: (b, i, 0))
    return pl.pallas_call(
        functools.partial(_post_kernel, ff_chunk=ff_chunk),
        grid=(B, T // tm),
        in_specs=[
            tok(ATT_WIDTH), tok(M_WIDTH), tok(M_WIDTH), tok(M_WIDTH), tok(G_WIDTH), tok(D_MODEL),
            pl.BlockSpec((1, 1, 6 * D_MODEL), mod_idx),
            _resident((D_MODEL, D_MODEL)), _resident((1, D_MODEL)), _resident((1, D_MODEL)),
            _resident((D_MODEL, D_FF)), _resident((D_FF, D_MODEL)),
            _resident((1, D_MODEL)), _resident((1, D_MODEL)),
        ],
        out_specs=tok(D_MODEL),
        out_shape=jax.ShapeDtypeStruct((B, T, D_MODEL), F32),
        compiler_params=pltpu.CompilerParams(
            dimension_semantics=("parallel", "parallel"), vmem_limit_bytes=VMEM_LIMIT),
        name="post",
    )(att, hf, hb, so, go, x, mod, wo, l1g, l1b, w1, w2, l2g, l2b)


def _rope_tables(T):
    rows = T // GRID_W
    r, cidx = jnp.meshgrid(jnp.arange(rows, dtype=F32), jnp.arange(GRID_W, dtype=F32), indexing='ij')
    n_freq = HEAD_DIM // 4
    inv = ROPE_BASE ** (-jnp.arange(n_freq, dtype=F32) / n_freq)
    ang_r = (r.reshape(-1)[:, None] * inv[None, :]).T
    ang_c = (cidx.reshape(-1)[:, None] * inv[None, :]).T
    cos = jnp.concatenate([jnp.cos(ang_r)] * 2 + [jnp.cos(ang_c)] * 2, axis=0)
    sin = jnp.concatenate([-jnp.sin(ang_r), jnp.sin(ang_r), -jnp.sin(ang_c), jnp.sin(ang_c)], axis=0)
    return cos, sin


def _with_ones_rows(vT):
    ones = jnp.ones(vT.shape[:-2] + (BF16_SUBLANES, vT.shape[-1]), vT.dtype)
    return jnp.concatenate([vT, ones], axis=-2)


def _pack_state(C, n, m):
    B = C.shape[0]
    ns = 2 * M_HEADS
    CT = jnp.swapaxes(C.reshape(B, ns, M_DIM, M_DIM), -1, -2).astype(F32)
    n = jnp.broadcast_to(n.reshape(B, ns, 1, M_DIM).astype(F32), (B, ns, M_DIM, M_DIM))
    m = jnp.broadcast_to(m.reshape(B, ns, 1, 1).astype(F32), (B, ns, 1, LANES))
    return jnp.concatenate([CT, n], axis=-2), m


def kernel(x_prompt, x_sample, c, cache_attn_k, cache_attn_v, state_mlstm_C, state_mlstm_n, state_mlstm_m,
           c_ctx, w_mod, b_mod, w_in, q_norm_g, k_norm_g, mlstm_gate_b, gmlp_ln_g, gmlp_ln_b, gmlp_ws,
           gmlp_bs, w_out, ln1_g, ln1_b, w_ff1, w_ff2, ln2_g, ln2_b):
    BP, TP, _ = x_prompt.shape
    BS, TS, _ = x_sample.shape
    n_mod = 16
    cvec = jnp.zeros((n_mod, D_MODEL), F32).at[0].set(c_ctx).at[1:1 + BS].set(c)
    mod_all = _modulation(cvec, w_mod, b_mod).reshape(DEPTH, n_mod, 1, 6 * D_MODEL)

    cos_s, sin_s = _rope_tables(TS)
    tm_p, tm_s = TP, 512

    xp, xs = x_prompt, x_sample
    ks_, vs_, Cs_, ns_, ms_ = [], [], [], [], []
    for l in range(DEPTH):
        wl = w_in[l]
        g0 = OFF_OM + M_WIDTH
        w_in_l = jnp.concatenate(
            [wl[:, :g0], wl[:, g0 + N_GATES:], wl[:, g0:g0 + N_GATES],
             jnp.zeros((D_MODEL, GATE_PAD - N_GATES), F32)], axis=1).astype(BF16)
        gate_b = jnp.zeros((1, GATE_PAD), F32).at[0, :N_GATES].set(mlstm_gate_b[l].reshape(-1))
        lng = gmlp_ln_g[l].reshape(1, G_WIDTH)
        lnb = gmlp_ln_b[l].reshape(1, G_WIDTH)
        ws = gmlp_ws[l].astype(BF16)
        bsx = jnp.repeat(gmlp_bs[l].T, G_CH, axis=1)
        wo = w_out[l].astype(BF16)
        w1 = w_ff1[l].astype(BF16)
        w2 = w_ff2[l].astype(BF16)
        l1g, l1b = ln1_g[l].reshape(1, -1), ln1_b[l].reshape(1, -1)
        l2g, l2b = ln2_g[l].reshape(1, -1), ln2_b[l].reshape(1, -1)
        mod = mod_all[l]

        def gains(tm):
            gq = jnp.broadcast_to((q_norm_g[l] * (HEAD_DIM ** -0.5 * LOG2_E))[:, None], (HEAD_DIM, tm))
            gk = jnp.broadcast_to(k_norm_g[l][:, None], (HEAD_DIM, tm))
            return gq, gk

        gq, gk = gains(tm_p)
        (qT, k, v1T, qmT, km, vmT, so, go, grow, k32, v32) = _inproj(
            xp, mod, 0, w_in_l, gq, gk, None, None, gate_b, lng, lnb, ws, bsx,
            rope=False, emit_kv32=True, tm=tm_p)
        att = _attention(qT, k, v1T, tq=TP, kc=TP, hpb=1)
        zero_state = _pack_state(jnp.zeros((BP, 2, M_HEADS, M_DIM, M_DIM), F32),
                                 jnp.zeros((BP, 2, M_HEADS, M_DIM), F32),
                                 jnp.zeros((BP, 2, M_HEADS), F32))
        hf, hb, s_fin, m_fin = _mlstm(qmT, km, vmT, grow, *zero_state)
        fold = lambda a: a.reshape(BP * TP // POST_TM, POST_TM, a.shape[-1])
        xp = _post(fold(att), fold(hf), fold(hb), fold(so), fold(go), fold(xp), mod, 0, wo, l1g, l1b, w1, w2,
                   l2g, l2b, tm=POST_TM, ff_chunk=1024).reshape(BP, TP, D_MODEL)
        ks_.append(k32.reshape(BP, TP, ATT_KV_HEADS, HEAD_DIM))
        vs_.append(v32.reshape(BP, TP, ATT_KV_HEADS, HEAD_DIM))
        Cs_.append(jnp.swapaxes(s_fin[:, :, :M_DIM, :], -1, -2).reshape(BP, 2, M_HEADS, M_DIM, M_DIM))
        ns_.append(s_fin[:, :, M_DIM, :].reshape(BP, 2, M_HEADS, M_DIM))
        ms_.append(m_fin[:, :, 0, 0].reshape(BP, 2, M_HEADS))

        gq, gk = gains(tm_s)
        (qT, k, v1T, qmT, km, vmT, so, go, grow) = _inproj(
            xs, mod, 1, w_in_l, gq, gk, cos_s, sin_s, gate_b, lng, lnb, ws, bsx,
            rope=True, emit_kv32=False, tm=tm_s)
        k_c = jnp.transpose(cache_attn_k[:, l], (0, 2, 1, 3)).astype(BF16)
        ones_col = jnp.zeros(k_c.shape[:-1] + (LANES - HEAD_DIM,), BF16).at[..., 0].set(1.0)
        k_c = jnp.concatenate([k_c, ones_col], axis=-1)
        vT_c = _with_ones_rows(jnp.transpose(cache_attn_v[:, l], (0, 2, 3, 1)).astype(BF16))
        k_all = jnp.concatenate([k_c, k], axis=2)
        v1T_all = jnp.concatenate([vT_c, v1T], axis=3)
        att = _attention(qT, k_all, v1T_all, tq=ATT_TQ, kc=ATT_KC, hpb=ATT_HPB)
        state = _pack_state(state_mlstm_C[:, l], state_mlstm_n[:, l], state_mlstm_m[:, l])
        hf, hb, _, _ = _mlstm(qmT, km, vmT, grow, *state)
        xs = _post(att, hf, hb, so, go, xs, mod, 1, wo, l1g, l1b, w1, w2, l2g, l2b, tm=POST_TM, ff_chunk=1024)

    return (xp, xs, jnp.stack(ks_, axis=1), jnp.stack(vs_, axis=1), jnp.stack(Cs_, axis=1),
            jnp.stack(ns_, axis=1), jnp.stack(ms_, axis=1))
```

```python
import functools

import jax
import jax.numpy as jnp
from jax import lax
from jax.experimental import pallas as pl
from jax.experimental.pallas import tpu as pltpu

D_MODEL = 1024
DEPTH = 2
GRID_W = 64
HEAD_DIM = 64
ATT_HEADS = 8
ATT_KV_HEADS = 2
ATT_GROUP = ATT_HEADS // ATT_KV_HEADS
ATT_WIDTH = ATT_HEADS * HEAD_DIM
KV_WIDTH = ATT_KV_HEADS * HEAD_DIM
M_HEADS = 4
M_DIM = 64
M_WIDTH = M_HEADS * M_DIM
G_GROUPS = 4
G_WIDTH = D_MODEL - ATT_WIDTH - M_WIDTH
G_CH = G_WIDTH // G_GROUPS
CHUNK = 128
D_FF = 4 * D_MODEL
ROPE_BASE = 10000.0
EPS = 1e-6
DEEPNORM_ALPHA = (2 * DEPTH) ** 0.25
N_GATES = 4 * M_HEADS

LANES = 128
BF16_SUBLANES = 16
GATE_PAD = LANES
OFF_QA = 0
OFF_KA = OFF_QA + ATT_WIDTH
OFF_VA = OFF_KA + KV_WIDTH
OFF_QM = OFF_VA + KV_WIDTH
OFF_KM = OFF_QM + M_WIDTH
OFF_VM = OFF_KM + M_WIDTH
OFF_OM = OFF_VM + M_WIDTH
OFF_UG = OFF_OM + M_WIDTH
OFF_VG = OFF_UG + G_WIDTH
OFF_GM = OFF_VG + G_WIDTH
PROJ_PAD = OFF_GM + GATE_PAD
V_ROWS = HEAD_DIM + BF16_SUBLANES
VMEM_LIMIT = 56 * 1024 * 1024

ATT_TQ, ATT_KC, ATT_HPB = 2048, 256, 2
POST_TM = 1024
MLSTM_BATCH = 4
LOG2_E = 1.4426950408889634
BOUND_MARGIN = 1.0 + 2.0 ** -6
MAX_FAST_SHIFT = 60.0

F32 = jnp.float32
BF16 = jnp.bfloat16


def _dot(a, b):
    return jnp.dot(a, b, preferred_element_type=F32)


def _resident(shape):
    nd = len(shape)
    return pl.BlockSpec(shape, lambda *_: (0,) * nd, pipeline_mode=pl.Buffered(1))


def _sigmoid(x):
    return 1.0 / (1.0 + jnp.exp(-x))


def _log_sigmoid(x):
    return jnp.minimum(x, 0.0) - jnp.log1p(jnp.exp(-jnp.abs(x)))


def _layer_norm(x, g, b):
    mu = jnp.mean(x, axis=-1, keepdims=True)
    xc = x - mu
    var = jnp.mean(xc * xc, axis=-1, keepdims=True)
    return xc * lax.rsqrt(var + EPS) * g + b


def _mod_kernel(c_ref, w_ref, b_ref, o_ref):
    c = c_ref[...]
    s = (c * _sigmoid(c)).astype(BF16)
    o_ref[0] = _dot(s, w_ref[0].astype(BF16)) + b_ref[0]


def _modulation(cvec, w_mod, b_mod):
    n = cvec.shape[0]
    bn = 1536
    return pl.pallas_call(
        _mod_kernel,
        grid=(DEPTH, 6 * D_MODEL // bn),
        in_specs=[
            pl.BlockSpec((n, D_MODEL), lambda l, j: (0, 0)),
            pl.BlockSpec((1, D_MODEL, bn), lambda l, j: (l, 0, j)),
            pl.BlockSpec((1, 1, bn), lambda l, j: (l, 0, j)),
        ],
        out_specs=pl.BlockSpec((1, n, bn), lambda l, j: (l, 0, j)),
        out_shape=jax.ShapeDtypeStruct((DEPTH, n, 6 * D_MODEL), F32),
        compiler_params=pltpu.CompilerParams(
            dimension_semantics=("arbitrary", "arbitrary"), vmem_limit_bytes=VMEM_LIMIT),
        name="modulation",
    )(cvec, w_mod, b_mod.reshape(DEPTH, 1, 6 * D_MODEL))


def _norm_rope_heads(zT, gain, cos, sin, n_heads):
    outs = []
    for hd in range(n_heads):
        t = zT[hd * HEAD_DIM:(hd + 1) * HEAD_DIM, :]
        ms = jnp.mean(t * t, axis=0, keepdims=True)
        y = t * lax.rsqrt(ms + EPS) * gain
        if cos is not None:
            q = HEAD_DIM // 4
            sw = jnp.concatenate([y[q:2 * q], y[0:q], y[3 * q:4 * q], y[2 * q:3 * q]], axis=0)
            y = y * cos + sw * sin
        outs.append(y)
    return outs


def _inproj_kernel(*refs, rope, emit_kv32, tm):
    it = iter(refs)
    x_ref, mod_ref, w_ref, gq_ref, gk_ref = (next(it) for _ in range(5))
    cos_ref, sin_ref = (next(it), next(it)) if rope else (None, None)
    gb_ref, lng_ref, lnb_ref, ws_ref, bsx_ref = (next(it) for _ in range(5))
    qT_ref, k_ref, vT_ref, qmT_ref, km_ref, vmT_ref, so_ref, go_ref, grow_ref = (
        next(it) for _ in range(9))
    k32_ref, v32_ref = (next(it), next(it)) if emit_kv32 else (None, None)

    x = x_ref[0]
    mod = mod_ref[0]
    sh1 = mod[:, 0:D_MODEL]
    sc1 = mod[:, D_MODEL:2 * D_MODEL]
    h = (x * (1.0 + sc1) + sh1).astype(BF16)

    cos = cos_ref[...] if rope else None
    sin = sin_ref[...] if rope else None

    zvg = _dot(h, w_ref[:, OFF_VG:OFF_VG + G_WIDTH])
    zu = _dot(h, w_ref[:, OFF_UG:OFF_UG + G_WIDTH])
    vn = _layer_norm(zvg, lng_ref[...], lnb_ref[...])

    zq = _dot(h, w_ref[:, OFF_QA:OFF_QA + ATT_WIDTH])
    q_heads = _norm_rope_heads(zq.T, gq_ref[...], cos, sin, ATT_HEADS)
    for hd in range(ATT_HEADS):
        qT_ref[0, hd] = q_heads[hd].astype(BF16)

    zk = _dot(h, w_ref[:, OFF_KA:OFF_KA + KV_WIDTH])
    k_heads = _norm_rope_heads(zk.T, gk_ref[...], cos, sin, ATT_KV_HEADS)
    kk = jnp.concatenate(k_heads, axis=0).T
    ones_col = (lax.broadcasted_iota(jnp.int32, (tm, LANES - HEAD_DIM), 1) == 0).astype(F32)
    for hd in range(ATT_KV_HEADS):
        k_ref[0, hd] = jnp.concatenate(
            [kk[:, hd * HEAD_DIM:(hd + 1) * HEAD_DIM], ones_col], axis=1).astype(BF16)
    zv = _dot(h, w_ref[:, OFF_VA:OFF_VA + KV_WIDTH])
    vT = zv.T
    ones_rows = jnp.ones((BF16_SUBLANES, tm), BF16)
    for hd in range(ATT_KV_HEADS):
        vT_ref[0, hd] = jnp.concatenate(
            [vT[hd * HEAD_DIM:(hd + 1) * HEAD_DIM].astype(BF16), ones_rows], axis=0)
    if emit_kv32:
        k32_ref[0] = kk
        v32_ref[0] = zv

    zqmT = _dot(h, w_ref[:, OFF_QM:OFF_QM + M_WIDTH]).T
    zkm = _dot(h, w_ref[:, OFF_KM:OFF_KM + M_WIDTH]) * (M_DIM ** -0.5)
    zvmT = _dot(h, w_ref[:, OFF_VM:OFF_VM + M_WIDTH]).T
    for hd in range(M_HEADS):
        qmT_ref[0, hd] = zqmT[hd * M_DIM:(hd + 1) * M_DIM, :].astype(BF16)
        km_ref[0, hd] = zkm[:, hd * M_DIM:(hd + 1) * M_DIM].astype(BF16)
        vmT_ref[0, hd] = zvmT[hd * M_DIM:(hd + 1) * M_DIM, :].astype(BF16)
    zo = _dot(h, w_ref[:, OFF_OM:OFF_OM + M_WIDTH])
    so_ref[0] = _sigmoid(zo).astype(BF16)

    zg = _dot(h, w_ref[:, OFF_GM:OFF_GM + GATE_PAD]) + gb_ref[...]
    lane = lax.broadcasted_iota(jnp.int32, zg.shape, 1)
    is_forget = (lane % (2 * M_HEADS)) >= M_HEADS
    grow_ref[0] = jnp.where(is_forget, _log_sigmoid(zg), zg).T[0:N_GATES, :]

    group = lax.broadcasted_iota(jnp.int32, (CHUNK, G_WIDTH), 1) // G_CH
    bsx = bsx_ref[...]
    s_chunks = []
    for j in range(tm // CHUNK):
        vc = vn[j * CHUNK:(j + 1) * CHUNK, :]
        acc = bsx
        for g in range(G_GROUPS):
            acc = acc + _dot(ws_ref[g], jnp.where(group == g, vc, 0.0).astype(BF16))
        s_chunks.append(acc)
    s = jnp.concatenate(s_chunks, axis=0) if len(s_chunks) > 1 else s_chunks[0]
    go_ref[0] = (zu * s).astype(BF16)


def _inproj(x, mod, mod_row, w_in, gq, gk, cos, sin, gate_b, lng, lnb, ws, bsx, *, rope, emit_kv32, tm):
    B, T, _ = x.shape
    nt = T // tm
    mod_idx = (lambda b, i: (b + mod_row, 0, 0)) if mod_row else (lambda b, i: (0, 0, 0))
    in_specs = [
        pl.BlockSpec((1, tm, D_MODEL), lambda b, i: (b, i, 0)),
        pl.BlockSpec((1, 1, 6 * D_MODEL), mod_idx),
        _resident((D_MODEL, PROJ_PAD)),
        _resident((HEAD_DIM, tm)),
        _resident((HEAD_DIM, tm)),
    ]
    args = [x, mod, w_in, gq, gk]
    if rope:
        in_specs += [pl.BlockSpec((HEAD_DIM, tm), lambda b, i: (0, i))] * 2
        args += [cos, sin]
    in_specs += [
        _resident((1, GATE_PAD)),
        _resident((1, G_WIDTH)),
        _resident((1, G_WIDTH)),
        _resident((G_GROUPS, CHUNK, CHUNK)),
        _resident((CHUNK, G_WIDTH)),
    ]
    args += [gate_b, lng, lnb, ws, bsx]
    out_shape = [
        jax.ShapeDtypeStruct((B, ATT_HEADS, HEAD_DIM, T), BF16),
        jax.ShapeDtypeStruct((B, ATT_KV_HEADS, T, LANES), BF16),
        jax.ShapeDtypeStruct((B, ATT_KV_HEADS, V_ROWS, T), BF16),
        jax.ShapeDtypeStruct((B, M_HEADS, M_DIM, T), BF16),
        jax.ShapeDtypeStruct((B, M_HEADS, T, M_DIM), BF16),
        jax.ShapeDtypeStruct((B, M_HEADS, M_DIM, T), BF16),
        jax.ShapeDtypeStruct((B, T, M_WIDTH), BF16),
        jax.ShapeDtypeStruct((B, T, G_WIDTH), BF16),
        jax.ShapeDtypeStruct((B, N_GATES, T), F32),
    ]
    out_specs = [
        pl.BlockSpec((1, ATT_HEADS, HEAD_DIM, tm), lambda b, i: (b, 0, 0, i)),
        pl.BlockSpec((1, ATT_KV_HEADS, tm, LANES), lambda b, i: (b, 0, i, 0)),
        pl.BlockSpec((1, ATT_KV_HEADS, V_ROWS, tm), lambda b, i: (b, 0, 0, i)),
        pl.BlockSpec((1, M_HEADS, M_DIM, tm), lambda b, i: (b, 0, 0, i)),
        pl.BlockSpec((1, M_HEADS, tm, M_DIM), lambda b, i: (b, 0, i, 0)),
        pl.BlockSpec((1, M_HEADS, M_DIM, tm), lambda b, i: (b, 0, 0, i)),
        pl.BlockSpec((1, tm, M_WIDTH), lambda b, i: (b, i, 0)),
        pl.BlockSpec((1, tm, G_WIDTH), lambda b, i: (b, i, 0)),
        pl.BlockSpec((1, N_GATES, tm), lambda b, i: (b, 0, i)),
    ]
    if emit_kv32:
        out_shape += [jax.ShapeDtypeStruct((B, T, KV_WIDTH), F32)] * 2
        out_specs += [pl.BlockSpec((1, tm, KV_WIDTH), lambda b, i: (b, i, 0))] * 2
    return pl.pallas_call(
        functools.partial(_inproj_kernel, rope=rope, emit_kv32=emit_kv32, tm=tm),
        grid=(B, nt),
        in_specs=in_specs,
        out_specs=out_specs,
        out_shape=out_shape,
        compiler_params=pltpu.CompilerParams(
            dimension_semantics=("parallel", "parallel"), vmem_limit_bytes=VMEM_LIMIT),
        name="inproj",
    )(*args)


def _attn_kernel(qT_ref, *refs, seg_keys, kc, tq, hpb):
    n_seg = len(seg_keys)
    k1_refs, v1T_refs = refs[:n_seg], refs[n_seg:2 * n_seg]
    o_ref, oT_scr, q1_scr, kmax_scr, acc_scr, m_scr = refs[2 * n_seg:]
    chunks = [(s, off) for s, n in enumerate(seg_keys) for off in range(0, n, kc)]
    nk = len(chunks)
    i = pl.program_id(2)

    @pl.when(i == 0)
    def _():
        kmax2 = None
        for k1_ref in k1_refs:
            kf = k1_ref[0, 0].astype(F32)
            n2 = jnp.max(jnp.sum(kf * kf, axis=1, keepdims=True) - 1.0, axis=0, keepdims=True)
            kmax2 = n2 if kmax2 is None else jnp.maximum(kmax2, n2)
        kmax_scr[...] = jnp.broadcast_to(kmax2, kmax_scr.shape)

    kmax2 = kmax_scr[0:1, 0:1]
    row0 = lax.broadcasted_iota(jnp.int32, (BF16_SUBLANES, tq), 0) == 0
    pad = jnp.zeros((LANES - HEAD_DIM - BF16_SUBLANES, tq), BF16)
    cmax = None
    for g in range(ATT_GROUP):
        qT = qT_ref[0, g]
        qf = qT.astype(F32)
        c = jnp.sqrt(jnp.sum(qf * qf, axis=0, keepdims=True) * kmax2) * BOUND_MARGIN
        shift = jnp.where(row0, -c, 0.0).astype(BF16)
        q1_scr[g] = jnp.concatenate([qT, shift, pad], axis=0)
        cg = jnp.max(c)
        cmax = cg if cmax is None else jnp.maximum(cmax, cg)
    fast = cmax <= MAX_FAST_SHIFT

    @pl.when(fast)
    def _():
        def heads(gi, carry):
            gs = [gi * hpb + a for a in range(hpb)]
            q1s = [q1_scr[g] for g in gs]

            def probs(j):
                s, off = chunks[j]
                kblk = k1_refs[s][0, 0, off:off + kc, :]
                return [jnp.exp2(_dot(kblk, q1s[a])).astype(BF16) for a in range(hpb)]

            accs = [None] * hpb
            p_next = probs(0)
            for j in range(nk):
                p_cur = p_next
                if j + 1 < nk:
                    p_next = probs(j + 1)
                s, off = chunks[j]
                vblk = v1T_refs[s][0, 0, :, off:off + kc]
                for a in range(hpb):
                    pv = _dot(vblk, p_cur[a])
                    accs[a] = pv if j == 0 else accs[a] + pv
            for a in range(hpb):
                oT_scr[gs[a]] = accs[a][0:HEAD_DIM] / accs[a][HEAD_DIM:HEAD_DIM + 1]
            return carry

        lax.fori_loop(0, ATT_GROUP // hpb, heads, 0)

    @pl.when(jnp.logical_not(fast))
    def _():
        def head(g, carry):
            qT = qT_ref[0, g]
            m_scr[...] = jnp.full(m_scr.shape, -jnp.inf, F32)
            acc_scr[...] = jnp.zeros(acc_scr.shape, F32)

            for k1_ref, v1T_ref, n in zip(k1_refs, v1T_refs, seg_keys):
                def chunk(j, c2, k1_ref=k1_ref, v1T_ref=v1T_ref):
                    ks = pl.multiple_of(j * kc, kc)
                    sT = _dot(k1_ref[0, 0, pl.ds(ks, kc), 0:HEAD_DIM], qT)
                    m_old = m_scr[...]
                    m_new = jnp.maximum(m_old, jnp.max(sT, axis=0, keepdims=True))
                    p = jnp.exp2(sT - m_new).astype(BF16)
                    acc_scr[...] = (jnp.exp2(m_old - m_new) * acc_scr[...]
                                    + _dot(v1T_ref[0, 0, :, pl.ds(ks, kc)], p))
                    m_scr[...] = m_new
                    return c2

                lax.fori_loop(0, n // kc, chunk, 0)
            acc = acc_scr[...]
            oT_scr[g] = acc[0:HEAD_DIM] / acc[HEAD_DIM:HEAD_DIM + 1]
            return carry

        lax.fori_loop(0, ATT_GROUP, head, 0)

    o_ref[0] = oT_scr[...].reshape(ATT_GROUP * HEAD_DIM, tq).T.astype(BF16)


def _attention(qT, k1_segs, v1T_segs, *, tq, kc, hpb):
    B, _, _, T = qT.shape
    seg_keys = tuple(k1.shape[2] for k1 in k1_segs)
    return pl.pallas_call(
        functools.partial(_attn_kernel, seg_keys=seg_keys, kc=kc, tq=tq, hpb=hpb),
        grid=(B, ATT_KV_HEADS, T // tq),
        in_specs=(
            [pl.BlockSpec((1, ATT_GROUP, HEAD_DIM, tq), lambda b, h, i: (b, h, 0, i))]
            + [pl.BlockSpec((1, 1, n, LANES), lambda b, h, i: (b, h, 0, 0)) for n in seg_keys]
            + [pl.BlockSpec((1, 1, V_ROWS, n), lambda b, h, i: (b, h, 0, 0)) for n in seg_keys]),
        out_specs=pl.BlockSpec((1, tq, ATT_GROUP * HEAD_DIM), lambda b, h, i: (b, i, h)),
        out_shape=jax.ShapeDtypeStruct((B, T, ATT_WIDTH), BF16),
        scratch_shapes=[
            pltpu.VMEM((ATT_GROUP, HEAD_DIM, tq), F32),
            pltpu.VMEM((ATT_GROUP, LANES, tq), BF16),
            pltpu.VMEM((8, LANES), F32),
            pltpu.VMEM((V_ROWS, tq), F32),
            pltpu.VMEM((1, tq), F32),
        ],
        compiler_params=pltpu.CompilerParams(
            dimension_semantics=("parallel", "parallel", "arbitrary"), vmem_limit_bytes=VMEM_LIMIT),
        name="attention",
    )(qT, *k1_segs, *v1T_segs)


def _mlstm_kernel(qTf_ref, kf_ref, vTf_ref, qTb_ref, kb_ref, vTb_ref, grf_ref, grb_ref,
                  s0_ref, m0_ref, hf_ref, hb_ref, sout_ref, mout_ref, s_scr, m_scr, *, nc, bb):
    c = pl.program_id(1)
    n_state = 2 * M_HEADS

    @pl.when(c == 0)
    def _():
        for bi in range(bb):
            s_scr[bi * n_state:(bi + 1) * n_state] = s0_ref[bi]
            m_scr[bi * n_state:(bi + 1) * n_state] = m0_ref[bi]

    si = lax.broadcasted_iota(jnp.int32, (CHUNK, CHUNK), 0)
    ti = lax.broadcasted_iota(jnp.int32, (CHUNK, CHUNK), 1)
    ones_v = jnp.ones((M_DIM, CHUNK), BF16)
    zrows = jnp.zeros((CHUNK - M_HEADS, CHUNK), F32)

    dirs = [(bi, d) for bi in range(bb) for d in range(2)]
    valids = [(si <= ti), (si >= ti)]
    gates = []
    for bi, d in dirs:
        gr = (grf_ref if d == 0 else grb_ref)[bi]
        tri = valids[d].astype(BF16)
        gr_hi = gr.astype(BF16)
        gr_lo = (gr - gr_hi.astype(F32)).astype(BF16)
        gates.append((gr, _dot(gr_hi, tri) + _dot(gr_lo, tri)))

    pre = []
    for bi, d in dirs:
        qT_ref, k_ref, vT_ref = (qTf_ref, kf_ref, vTf_ref) if d == 0 else (qTb_ref, kb_ref, vTb_ref)
        for hd in range(M_HEADS):
            qT = qT_ref[bi, hd]
            kh = k_ref[bi, hd]
            v1T = jnp.concatenate([vT_ref[bi, hd], ones_v], axis=0)
            pre.append(dict(qT=qT, kh=kh, v1T=v1T, kq=_dot(kh, qT)))

    for (bi, d), (gr, arow_all) in zip(dirs, gates):
        g0 = 2 * M_HEADS * d
        a_rows = arow_all[g0 + M_HEADS:g0 + 2 * M_HEADS, :]
        r_rows = gr[g0:g0 + M_HEADS, :] - a_rows
        r_cols = jnp.concatenate([r_rows, zrows], axis=0).T
        last = CHUNK - 1 if d == 0 else 0
        for hd in range(M_HEADS):
            e = pre[(2 * bi + d) * M_HEADS + hd]
            a_row = a_rows[hd:hd + 1, :]
            x = jnp.where(valids[d], r_cols[:, hd:hd + 1], -jnp.inf)
            R_row = jnp.max(x, axis=0, keepdims=True)
            R_L = jnp.broadcast_to(R_row[:, last:last + 1], (1, LANES))
            wk = jnp.exp(r_rows[hd:hd + 1, :] - R_L)
            e.update(a_row=a_row, R_row=R_row, R_L=R_L, a_L=a_row[:, last:last + 1],
                     kqd=e["kq"] * jnp.exp(x - R_row),
                     uT=_dot((e["v1T"].astype(F32) * wk).astype(BF16), e["kh"]))

    outs = []
    for j, e in enumerate(pre):
        m = m_scr[j]
        st = s_scr[j]
        M_row = jnp.maximum(m, e["R_row"])
        s0T = (e["kqd"] * jnp.exp(e["R_row"] - M_row)).astype(BF16)
        qs = (e["qT"].astype(F32) * jnp.exp(m - M_row)).astype(BF16)
        lhs = jnp.concatenate([e["v1T"], st.astype(BF16)], axis=1)
        tot = _dot(lhs, jnp.concatenate([s0T, qs], axis=0))
        den = jnp.maximum(jnp.abs(tot[M_DIM:2 * M_DIM]), jnp.exp(-(e["a_row"] + M_row)))
        outs.append(tot[0:M_DIM] / den)
        M_L = jnp.maximum(m, e["R_L"])
        s_scr[j] = jnp.exp(m - M_L)[:, 0:M_DIM] * st + jnp.exp(e["R_L"] - M_L)[:, 0:M_DIM] * e["uT"]
        m_scr[j] = jnp.broadcast_to(e["a_L"], (1, LANES)) + M_L
    for bi in range(bb):
        o = bi * n_state
        hf_ref[bi] = jnp.concatenate(outs[o:o + M_HEADS], axis=0).T
        hb_ref[bi] = jnp.concatenate(outs[o + M_HEADS:o + n_state], axis=0).T

    @pl.when(c == nc - 1)
    def _():
        for bi in range(bb):
            sout_ref[bi] = s_scr[bi * n_state:(bi + 1) * n_state]
            mout_ref[bi] = m_scr[bi * n_state:(bi + 1) * n_state]


def _mlstm(qmT, km, vmT, grow, s0, m0):
    B, _, _, T = qmT.shape
    nc = T // CHUNK
    fwd3 = lambda b, c: (b, 0, c, 0)
    bwd3 = lambda b, c: (b, 0, nc - 1 - c, 0)
    fwdT = lambda b, c: (b, 0, 0, c)
    bwdT = lambda b, c: (b, 0, 0, nc - 1 - c)
    bb = MLSTM_BATCH
    rspec = lambda im: pl.BlockSpec((bb, M_HEADS, CHUNK, M_DIM), im)
    tspec = lambda im: pl.BlockSpec((bb, M_HEADS, M_DIM, CHUNK), im)
    n_state = 2 * M_HEADS
    return pl.pallas_call(
        functools.partial(_mlstm_kernel, nc=nc, bb=bb),
        grid=(B // bb, nc),
        in_specs=[
            tspec(fwdT), rspec(fwd3), tspec(fwdT),
            tspec(bwdT), rspec(bwd3), tspec(bwdT),
            pl.BlockSpec((bb, N_GATES, CHUNK), lambda b, c: (b, 0, c)),
            pl.BlockSpec((bb, N_GATES, CHUNK), lambda b, c: (b, 0, nc - 1 - c)),
            pl.BlockSpec((bb, n_state, 2 * M_DIM, M_DIM), lambda b, c: (b, 0, 0, 0)),
            pl.BlockSpec((bb, n_state, 1, LANES), lambda b, c: (b, 0, 0, 0)),
        ],
        out_specs=[
            pl.BlockSpec((bb, CHUNK, M_WIDTH), lambda b, c: (b, c, 0)),
            pl.BlockSpec((bb, CHUNK, M_WIDTH), lambda b, c: (b, nc - 1 - c, 0)),
            pl.BlockSpec((bb, n_state, 2 * M_DIM, M_DIM), lambda b, c: (b, 0, 0, 0)),
            pl.BlockSpec((bb, n_state, 1, LANES), lambda b, c: (b, 0, 0, 0)),
        ],
        out_shape=[
            jax.ShapeDtypeStruct((B, T, M_WIDTH), F32),
            jax.ShapeDtypeStruct((B, T, M_WIDTH), F32),
            jax.ShapeDtypeStruct((B, n_state, 2 * M_DIM, M_DIM), F32),
            jax.ShapeDtypeStruct((B, n_state, 1, LANES), F32),
        ],
        scratch_shapes=[pltpu.VMEM((bb * n_state, 2 * M_DIM, M_DIM), F32),
                        pltpu.VMEM((bb * n_state, 1, LANES), F32)],
        compiler_params=pltpu.CompilerParams(
            dimension_semantics=("parallel", "arbitrary"), vmem_limit_bytes=VMEM_LIMIT),
        name="mlstm",
    )(qmT, km, vmT, qmT, km, vmT, grow, grow, s0, m0)


def _post_kernel(att_ref, hf_ref, hb_ref, so_ref, go_ref, x_ref, mod_ref, wo_ref, l1g_ref, l1b_ref,
                 w1_ref, w2_ref, l2g_ref, l2b_ref, o_ref, *, ff_chunk):
    mod = mod_ref[0]
    g1 = mod[:, 2 * D_MODEL:3 * D_MODEL]
    sh2 = mod[:, 3 * D_MODEL:4 * D_MODEL]
    sc2 = mod[:, 4 * D_MODEL:5 * D_MODEL]
    g2 = mod[:, 5 * D_MODEL:6 * D_MODEL]
    half = att_ref.shape[1] // 2
    rows = [slice(r * half, (r + 1) * half) for r in range(2)]
    ys = []
    for r in rows:
        m_out = (so_ref[0, r].astype(F32) * (hf_ref[0, r] + hb_ref[0, r])).astype(BF16)
        cat = jnp.concatenate([att_ref[0, r], m_out, go_ref[0, r]], axis=1)
        ys.append(_dot(cat, wo_ref[...]))
    for r, y in zip(rows, ys):
        x1 = _layer_norm(DEEPNORM_ALPHA * x_ref[0, r] + g1 * y, l1g_ref[...], l1b_ref[...])
        h2 = (x1 * (1.0 + sc2) + sh2).astype(BF16)
        f = None
        for j in range(D_FF // ff_chunk):
            a = jnp.maximum(_dot(h2, w1_ref[:, j * ff_chunk:(j + 1) * ff_chunk]), 0.0)
            part = _dot((a * a).astype(BF16), w2_ref[j * ff_chunk:(j + 1) * ff_chunk, :])
            f = part if f is None else f + part
        o_ref[0, r] = _layer_norm(DEEPNORM_ALPHA * x1 + g2 * f, l2g_ref[...], l2b_ref[...])


def _post(att, hf, hb, so, go, x, mod, mod_row, wo, l1g, l1b, w1, w2, l2g, l2b, *, tm, ff_chunk):
    B, T, _ = x.shape
    mod_idx = (lambda b, i: (b + mod_row, 0, 0)) if mod_row else (lambda b, i: (0, 0, 0))
    tok = lambda w: pl.BlockSpec((1, tm, w), lambda b, i: (b, i, 0))
    return pl.pallas_call(
        functools.partial(_post_kernel, ff_chunk=ff_chunk),
        grid=(B, T // tm),
        in_specs=[
            tok(ATT_WIDTH), tok(M_WIDTH), tok(M_WIDTH), tok(M_WIDTH), tok(G_WIDTH), tok(D_MODEL),
            pl.BlockSpec((1, 1, 6 * D_MODEL), mod_idx),
            _resident((D_MODEL, D_MODEL)), _resident((1, D_MODEL)), _resident((1, D_MODEL)),
            _resident((D_MODEL, D_FF)), _resident((D_FF, D_MODEL)),
            _resident((1, D_MODEL)), _resident((1, D_MODEL)),
        ],
        out_specs=tok(D_MODEL),
        out_shape=jax.ShapeDtypeStruct((B, T, D_MODEL), F32),
        compiler_params=pltpu.CompilerParams(
            dimension_semantics=("parallel", "parallel"), vmem_limit_bytes=VMEM_LIMIT),
        name="post",
    )(att, hf, hb, so, go, x, mod, wo, l1g, l1b, w1, w2, l2g, l2b)


def _rope_tables(T):
    rows = T // GRID_W
    r, cidx = jnp.meshgrid(jnp.arange(rows, dtype=F32), jnp.arange(GRID_W, dtype=F32), indexing='ij')
    n_freq = HEAD_DIM // 4
    inv = ROPE_BASE ** (-jnp.arange(n_freq, dtype=F32) / n_freq)
    ang_r = (r.reshape(-1)[:, None] * inv[None, :]).T
    ang_c = (cidx.reshape(-1)[:, None] * inv[None, :]).T
    cos = jnp.concatenate([jnp.cos(ang_r)] * 2 + [jnp.cos(ang_c)] * 2, axis=0)
    sin = jnp.concatenate([-jnp.sin(ang_r), jnp.sin(ang_r), -jnp.sin(ang_c), jnp.sin(ang_c)], axis=0)
    return cos, sin


def _with_ones_rows(vT):
    ones = jnp.ones(vT.shape[:-2] + (BF16_SUBLANES, vT.shape[-1]), vT.dtype)
    return jnp.concatenate([vT, ones], axis=-2)


def _pack_state(C, n, m):
    B = C.shape[0]
    ns = 2 * M_HEADS
    CT = jnp.swapaxes(C.reshape(B, ns, M_DIM, M_DIM), -1, -2).astype(F32)
    n = jnp.broadcast_to(n.reshape(B, ns, 1, M_DIM).astype(F32), (B, ns, M_DIM, M_DIM))
    m = jnp.broadcast_to(m.reshape(B, ns, 1, 1).astype(F32), (B, ns, 1, LANES))
    return jnp.concatenate([CT, n], axis=-2), m


def kernel(x_prompt, x_sample, c, cache_attn_k, cache_attn_v, state_mlstm_C, state_mlstm_n, state_mlstm_m,
           c_ctx, w_mod, b_mod, w_in, q_norm_g, k_norm_g, mlstm_gate_b, gmlp_ln_g, gmlp_ln_b, gmlp_ws,
           gmlp_bs, w_out, ln1_g, ln1_b, w_ff1, w_ff2, ln2_g, ln2_b):
    BP, TP, _ = x_prompt.shape
    BS, TS, _ = x_sample.shape
    n_mod = 16
    cvec = jnp.zeros((n_mod, D_MODEL), F32).at[0].set(c_ctx).at[1:1 + BS].set(c)
    mod_all = _modulation(cvec, w_mod, b_mod).reshape(DEPTH, n_mod, 1, 6 * D_MODEL)

    cos_s, sin_s = _rope_tables(TS)
    tm_p, tm_s = TP, 512

    xp, xs = x_prompt, x_sample
    ks_, vs_, Cs_, ns_, ms_ = [], [], [], [], []
    for l in range(DEPTH):
        wl = w_in[l]
        g0 = OFF_OM + M_WIDTH
        w_in_l = jnp.concatenate(
            [wl[:, :g0], wl[:, g0 + N_GATES:], wl[:, g0:g0 + N_GATES],
             jnp.zeros((D_MODEL, GATE_PAD - N_GATES), F32)], axis=1).astype(BF16)
        gate_b = jnp.zeros((1, GATE_PAD), F32).at[0, :N_GATES].set(mlstm_gate_b[l].reshape(-1))
        lng = gmlp_ln_g[l].reshape(1, G_WIDTH)
        lnb = gmlp_ln_b[l].reshape(1, G_WIDTH)
        ws = gmlp_ws[l].astype(BF16)
        bsx = jnp.repeat(gmlp_bs[l].T, G_CH, axis=1)
        wo = w_out[l].astype(BF16)
        w1 = w_ff1[l].astype(BF16)
        w2 = w_ff2[l].astype(BF16)
        l1g, l1b = ln1_g[l].reshape(1, -1), ln1_b[l].reshape(1, -1)
        l2g, l2b = ln2_g[l].reshape(1, -1), ln2_b[l].reshape(1, -1)
        mod = mod_all[l]

        def gains(tm):
            gq = jnp.broadcast_to((q_norm_g[l] * (HEAD_DIM ** -0.5 * LOG2_E))[:, None], (HEAD_DIM, tm))
            gk = jnp.broadcast_to(k_norm_g[l][:, None], (HEAD_DIM, tm))
            return gq, gk

        gq, gk = gains(tm_p)
        (qT, k, v1T, qmT, km, vmT, so, go, grow, k32, v32) = _inproj(
            xp, mod, 0, w_in_l, gq, gk, None, None, gate_b, lng, lnb, ws, bsx,
            rope=False, emit_kv32=True, tm=tm_p)
        att = _attention(qT, [k], [v1T], tq=TP, kc=TP, hpb=ATT_GROUP)
        zero_state = _pack_state(jnp.zeros((BP, 2, M_HEADS, M_DIM, M_DIM), F32),
                                 jnp.zeros((BP, 2, M_HEADS, M_DIM), F32),
                                 jnp.zeros((BP, 2, M_HEADS), F32))
        hf, hb, s_fin, m_fin = _mlstm(qmT, km, vmT, grow, *zero_state)
        fold = lambda a: a.reshape(BP * TP // POST_TM, POST_TM, a.shape[-1])
        xp = _post(fold(att), fold(hf), fold(hb), fold(so), fold(go), fold(xp), mod, 0, wo, l1g, l1b, w1, w2,
                   l2g, l2b, tm=POST_TM, ff_chunk=1024).reshape(BP, TP, D_MODEL)
        ks_.append(k32.reshape(BP, TP, ATT_KV_HEADS, HEAD_DIM))
        vs_.append(v32.reshape(BP, TP, ATT_KV_HEADS, HEAD_DIM))
        Cs_.append(jnp.swapaxes(s_fin[:, :, :M_DIM, :], -1, -2).reshape(BP, 2, M_HEADS, M_DIM, M_DIM))
        ns_.append(s_fin[:, :, M_DIM, :].reshape(BP, 2, M_HEADS, M_DIM))
        ms_.append(m_fin[:, :, 0, 0].reshape(BP, 2, M_HEADS))

        gq, gk = gains(tm_s)
        (qT, k, v1T, qmT, km, vmT, so, go, grow) = _inproj(
            xs, mod, 1, w_in_l, gq, gk, cos_s, sin_s, gate_b, lng, lnb, ws, bsx,
            rope=True, emit_kv32=False, tm=tm_s)
        k_c = jnp.transpose(cache_attn_k[:, l], (0, 2, 1, 3)).astype(BF16)
        ones_col = jnp.zeros(k_c.shape[:-1] + (LANES - HEAD_DIM,), BF16).at[..., 0].set(1.0)
        k_c = jnp.concatenate([k_c, ones_col], axis=-1)
        vT_c = _with_ones_rows(jnp.transpose(cache_attn_v[:, l], (0, 2, 3, 1)).astype(BF16))
        att = _attention(qT, [k_c, k], [vT_c, v1T], tq=ATT_TQ, kc=ATT_KC, hpb=ATT_HPB)
        state = _pack_state(state_mlstm_C[:, l], state_mlstm_n[:, l], state_mlstm_m[:, l])
        hf, hb, _, _ = _mlstm(qmT, km, vmT, grow, *state)
        xs = _post(att, hf, hb, so, go, xs, mod, 1, wo, l1g, l1b, w1, w2, l2g, l2b, tm=POST_TM, ff_chunk=1024)

    return (xp, xs, jnp.stack(ks_, axis=1), jnp.stack(vs_, axis=1), jnp.stack(Cs_, axis=1),
            jnp.stack(ns_, axis=1), jnp.stack(ms_, axis=1))
```

```python
import functools

import jax
import jax.numpy as jnp
from jax import lax
from jax.experimental import pallas as pl
from jax.experimental.pallas import tpu as pltpu

D_MODEL = 1024
DEPTH = 2
GRID_W = 64
HEAD_DIM = 64
ATT_HEADS = 8
ATT_KV_HEADS = 2
ATT_GROUP = ATT_HEADS // ATT_KV_HEADS
ATT_WIDTH = ATT_HEADS * HEAD_DIM
KV_WIDTH = ATT_KV_HEADS * HEAD_DIM
M_HEADS = 4
M_DIM = 64
M_WIDTH = M_HEADS * M_DIM
G_GROUPS = 4
G_WIDTH = D_MODEL - ATT_WIDTH - M_WIDTH
G_CH = G_WIDTH // G_GROUPS
CHUNK = 128
D_FF = 4 * D_MODEL
ROPE_BASE = 10000.0
EPS = 1e-6
DEEPNORM_ALPHA = (2 * DEPTH) ** 0.25
N_GATES = 4 * M_HEADS

LANES = 128
F32_SUBLANES = 8
BF16_SUBLANES = 16
GATE_PAD = LANES
OFF_QA = 0
OFF_KA = OFF_QA + ATT_WIDTH
OFF_VA = OFF_KA + KV_WIDTH
OFF_QM = OFF_VA + KV_WIDTH
OFF_KM = OFF_QM + M_WIDTH
OFF_VM = OFF_KM + M_WIDTH
OFF_OM = OFF_VM + M_WIDTH
OFF_UG = OFF_OM + M_WIDTH
OFF_VG = OFF_UG + G_WIDTH
OFF_GM = OFF_VG + G_WIDTH
PROJ_PAD = OFF_GM + GATE_PAD
V_ROWS = HEAD_DIM + BF16_SUBLANES
VMEM_LIMIT = 56 * 1024 * 1024

ATT_TQ, ATT_KC, ATT_HPB = 2048, 256, 2
INPROJ_TM = 1024
POST_TM = 1024
MOD_BN = 1536
MOD_ROWS = 16
MLSTM_BATCH = 4
LOG2_E = 1.4426950408889634
BOUND_MARGIN = 1.0 + 2.0 ** -6
MAX_FAST_SHIFT = 60.0

F32 = jnp.float32
BF16 = jnp.bfloat16


def _dot(a, b):
    return jnp.dot(a, b, preferred_element_type=F32)


def _resident(shape):
    nd = len(shape)
    return pl.BlockSpec(shape, lambda *_: (0,) * nd, pipeline_mode=pl.Buffered(1))


def _sigmoid(x):
    return 1.0 / (1.0 + jnp.exp(-x))


def _log_sigmoid(x):
    return jnp.minimum(x, 0.0) - jnp.log1p(jnp.exp(-jnp.abs(x)))


def _layer_norm(x, g, b):
    mu = jnp.mean(x, axis=-1, keepdims=True)
    xc = x - mu
    var = jnp.mean(xc * xc, axis=-1, keepdims=True)
    return xc * lax.rsqrt(var + EPS) * g + b


def _mod_kernel(c_ref, w_ref, b_ref, o_ref):
    c = c_ref[...]
    s = (c * _sigmoid(c)).astype(BF16)
    o_ref[0] = _dot(s, w_ref[0].astype(BF16)) + b_ref[0]


def _modulation(cvec, w_mod, b_mod):
    n = cvec.shape[0]
    bn = MOD_BN
    return pl.pallas_call(
        _mod_kernel,
        grid=(DEPTH, 6 * D_MODEL // bn),
        in_specs=[
            pl.BlockSpec((n, D_MODEL), lambda l, j: (0, 0)),
            pl.BlockSpec((1, D_MODEL, bn), lambda l, j: (l, 0, j)),
            pl.BlockSpec((1, 1, bn), lambda l, j: (l, 0, j)),
        ],
        out_specs=pl.BlockSpec((1, n, bn), lambda l, j: (l, 0, j)),
        out_shape=jax.ShapeDtypeStruct((DEPTH, n, 6 * D_MODEL), F32),
        compiler_params=pltpu.CompilerParams(
            dimension_semantics=("arbitrary", "arbitrary"), vmem_limit_bytes=VMEM_LIMIT),
        name="modulation",
    )(cvec, w_mod, b_mod.reshape(DEPTH, 1, 6 * D_MODEL))


def _norm_rope_heads(zT, gain, cos, sin, n_heads):
    outs = []
    for hd in range(n_heads):
        t = zT[hd * HEAD_DIM:(hd + 1) * HEAD_DIM, :]
        ms = jnp.mean(t * t, axis=0, keepdims=True)
        y = t * lax.rsqrt(ms + EPS) * gain
        if cos is not None:
            q = HEAD_DIM // 4
            sw = jnp.concatenate([y[q:2 * q], y[0:q], y[3 * q:4 * q], y[2 * q:3 * q]], axis=0)
            y = y * cos + sw * sin
        outs.append(y)
    return outs


def _inproj_kernel(*refs, rope, emit_kv32, tm):
    it = iter(refs)
    x_ref, mod_ref, w_ref, gq_ref, gk_ref = (next(it) for _ in range(5))
    cos_ref, sin_ref = (next(it), next(it)) if rope else (None, None)
    gb_ref, lng_ref, lnb_ref, ws_ref, bsx_ref = (next(it) for _ in range(5))
    qT_ref, k_ref, vT_ref, qmT_ref, km_ref, vmT_ref, so_ref, go_ref, grow_ref = (
        next(it) for _ in range(9))
    k32_ref, v32_ref = (next(it), next(it)) if emit_kv32 else (None, None)

    x = x_ref[0]
    mod = mod_ref[0]
    sh1 = mod[:, 0:D_MODEL]
    sc1 = mod[:, D_MODEL:2 * D_MODEL]
    h = (x * (1.0 + sc1) + sh1).astype(BF16)

    cos = cos_ref[...] if rope else None
    sin = sin_ref[...] if rope else None

    zvg = _dot(h, w_ref[:, OFF_VG:OFF_VG + G_WIDTH])
    zu = _dot(h, w_ref[:, OFF_UG:OFF_UG + G_WIDTH])
    vn = _layer_norm(zvg, lng_ref[...], lnb_ref[...])

    zq = _dot(h, w_ref[:, OFF_QA:OFF_QA + ATT_WIDTH])
    q_heads = _norm_rope_heads(zq.T, gq_ref[...], cos, sin, ATT_HEADS)
    for hd in range(ATT_HEADS):
        qT_ref[0, hd] = q_heads[hd].astype(BF16)

    zk = _dot(h, w_ref[:, OFF_KA:OFF_KA + KV_WIDTH])
    k_heads = _norm_rope_heads(zk.T, gk_ref[...], cos, sin, ATT_KV_HEADS)
    kk = jnp.concatenate(k_heads, axis=0).T
    ones_col = (lax.broadcasted_iota(jnp.int32, (tm, LANES - HEAD_DIM), 1) == 0).astype(F32)
    for hd in range(ATT_KV_HEADS):
        k_ref[0, hd] = jnp.concatenate(
            [kk[:, hd * HEAD_DIM:(hd + 1) * HEAD_DIM], ones_col], axis=1).astype(BF16)
    zv = _dot(h, w_ref[:, OFF_VA:OFF_VA + KV_WIDTH])
    vT = zv.T
    ones_rows = jnp.ones((BF16_SUBLANES, tm), BF16)
    for hd in range(ATT_KV_HEADS):
        vT_ref[0, hd] = jnp.concatenate(
            [vT[hd * HEAD_DIM:(hd + 1) * HEAD_DIM].astype(BF16), ones_rows], axis=0)
    if emit_kv32:
        k32_ref[0] = kk
        v32_ref[0] = zv

    zqmT = _dot(h, w_ref[:, OFF_QM:OFF_QM + M_WIDTH]).T
    zkm = _dot(h, w_ref[:, OFF_KM:OFF_KM + M_WIDTH]) * (M_DIM ** -0.5)
    zvmT = _dot(h, w_ref[:, OFF_VM:OFF_VM + M_WIDTH]).T
    for hd in range(M_HEADS):
        qmT_ref[0, hd] = zqmT[hd * M_DIM:(hd + 1) * M_DIM, :].astype(BF16)
        km_ref[0, hd] = zkm[:, hd * M_DIM:(hd + 1) * M_DIM].astype(BF16)
        vmT_ref[0, hd] = zvmT[hd * M_DIM:(hd + 1) * M_DIM, :].astype(BF16)
    zo = _dot(h, w_ref[:, OFF_OM:OFF_OM + M_WIDTH])
    so_ref[0] = _sigmoid(zo).astype(BF16)

    zg = _dot(h, w_ref[:, OFF_GM:OFF_GM + GATE_PAD]) + gb_ref[...]
    lane = lax.broadcasted_iota(jnp.int32, zg.shape, 1)
    is_forget = (lane % (2 * M_HEADS)) >= M_HEADS
    grow_ref[0] = jnp.where(is_forget, _log_sigmoid(zg), zg).T[0:N_GATES, :]

    group = lax.broadcasted_iota(jnp.int32, (CHUNK, G_WIDTH), 1) // G_CH
    bsx = bsx_ref[...]
    s_chunks = []
    for j in range(tm // CHUNK):
        vc = vn[j * CHUNK:(j + 1) * CHUNK, :]
        acc = bsx
        for g in range(G_GROUPS):
            acc = acc + _dot(ws_ref[g], jnp.where(group == g, vc, 0.0).astype(BF16))
        s_chunks.append(acc)
    s = jnp.concatenate(s_chunks, axis=0) if len(s_chunks) > 1 else s_chunks[0]
    go_ref[0] = (zu * s).astype(BF16)


def _inproj(x, mod, mod_row, w_in, gq, gk, cos, sin, gate_b, lng, lnb, ws, bsx, *, rope, emit_kv32, tm):
    B, T, _ = x.shape
    nt = T // tm
    mod_idx = (lambda b, i: (b + mod_row, 0, 0)) if mod_row else (lambda b, i: (0, 0, 0))
    in_specs = [
        pl.BlockSpec((1, tm, D_MODEL), lambda b, i: (b, i, 0)),
        pl.BlockSpec((1, 1, 6 * D_MODEL), mod_idx),
        _resident((D_MODEL, PROJ_PAD)),
        _resident((HEAD_DIM, tm)),
        _resident((HEAD_DIM, tm)),
    ]
    args = [x, mod, w_in, gq, gk]
    if rope:
        in_specs += [pl.BlockSpec((HEAD_DIM, tm), lambda b, i: (0, i))] * 2
        args += [cos, sin]
    in_specs += [
        _resident((1, GATE_PAD)),
        _resident((1, G_WIDTH)),
        _resident((1, G_WIDTH)),
        _resident((G_GROUPS, CHUNK, CHUNK)),
        _resident((CHUNK, G_WIDTH)),
    ]
    args += [gate_b, lng, lnb, ws, bsx]
    out_shape = [
        jax.ShapeDtypeStruct((B, ATT_HEADS, HEAD_DIM, T), BF16),
        jax.ShapeDtypeStruct((B, ATT_KV_HEADS, T, LANES), BF16),
        jax.ShapeDtypeStruct((B, ATT_KV_HEADS, V_ROWS, T), BF16),
        jax.ShapeDtypeStruct((B, M_HEADS, M_DIM, T), BF16),
        jax.ShapeDtypeStruct((B, M_HEADS, T, M_DIM), BF16),
        jax.ShapeDtypeStruct((B, M_HEADS, M_DIM, T), BF16),
        jax.ShapeDtypeStruct((B, T, M_WIDTH), BF16),
        jax.ShapeDtypeStruct((B, T, G_WIDTH), BF16),
        jax.ShapeDtypeStruct((B, N_GATES, T), F32),
    ]
    out_specs = [
        pl.BlockSpec((1, ATT_HEADS, HEAD_DIM, tm), lambda b, i: (b, 0, 0, i)),
        pl.BlockSpec((1, ATT_KV_HEADS, tm, LANES), lambda b, i: (b, 0, i, 0)),
        pl.BlockSpec((1, ATT_KV_HEADS, V_ROWS, tm), lambda b, i: (b, 0, 0, i)),
        pl.BlockSpec((1, M_HEADS, M_DIM, tm), lambda b, i: (b, 0, 0, i)),
        pl.BlockSpec((1, M_HEADS, tm, M_DIM), lambda b, i: (b, 0, i, 0)),
        pl.BlockSpec((1, M_HEADS, M_DIM, tm), lambda b, i: (b, 0, 0, i)),
        pl.BlockSpec((1, tm, M_WIDTH), lambda b, i: (b, i, 0)),
        pl.BlockSpec((1, tm, G_WIDTH), lambda b, i: (b, i, 0)),
        pl.BlockSpec((1, N_GATES, tm), lambda b, i: (b, 0, i)),
    ]
    if emit_kv32:
        out_shape += [jax.ShapeDtypeStruct((B, T, KV_WIDTH), F32)] * 2
        out_specs += [pl.BlockSpec((1, tm, KV_WIDTH), lambda b, i: (b, i, 0))] * 2
    return pl.pallas_call(
        functools.partial(_inproj_kernel, rope=rope, emit_kv32=emit_kv32, tm=tm),
        grid=(B, nt),
        in_specs=in_specs,
        out_specs=out_specs,
        out_shape=out_shape,
        compiler_params=pltpu.CompilerParams(
            dimension_semantics=("parallel", "parallel"), vmem_limit_bytes=VMEM_LIMIT),
        name="inproj",
    )(*args)


def _attn_kernel(qT_ref, *refs, seg_keys, kc, tq, hpb):
    n_seg = len(seg_keys)
    k1_refs, v1T_refs = refs[:n_seg], refs[n_seg:2 * n_seg]
    o_ref, oT_scr, q1_scr, kmax_scr, acc_scr, m_scr = refs[2 * n_seg:]
    chunks = [(s, off) for s, n in enumerate(seg_keys) for off in range(0, n, kc)]
    nk = len(chunks)
    i = pl.program_id(2)

    @pl.when(i == 0)
    def _():
        kmax2 = None
        for k1_ref in k1_refs:
            kf = k1_ref[0, 0].astype(F32)
            n2 = jnp.max(jnp.sum(kf * kf, axis=1, keepdims=True) - 1.0, axis=0, keepdims=True)
            kmax2 = n2 if kmax2 is None else jnp.maximum(kmax2, n2)
        kmax_scr[...] = jnp.broadcast_to(kmax2, kmax_scr.shape)

    kmax2 = kmax_scr[0:1, 0:1]
    row0 = lax.broadcasted_iota(jnp.int32, (BF16_SUBLANES, tq), 0) == 0
    pad = jnp.zeros((LANES - HEAD_DIM - BF16_SUBLANES, tq), BF16)
    cmax = None
    for g in range(ATT_GROUP):
        qT = qT_ref[0, g]
        qf = qT.astype(F32)
        c = jnp.sqrt(jnp.sum(qf * qf, axis=0, keepdims=True) * kmax2) * BOUND_MARGIN
        shift = jnp.where(row0, -c, 0.0).astype(BF16)
        q1_scr[g] = jnp.concatenate([qT, shift, pad], axis=0)
        cg = jnp.max(c)
        cmax = cg if cmax is None else jnp.maximum(cmax, cg)
    fast = cmax <= MAX_FAST_SHIFT

    @pl.when(fast)
    def _():
        def heads(gi, carry):
            gs = [gi * hpb + a for a in range(hpb)]
            q1s = [q1_scr[g] for g in gs]

            def probs(j):
                s, off = chunks[j]
                kblk = k1_refs[s][0, 0, off:off + kc, :]
                return [jnp.exp2(_dot(kblk, q1s[a])).astype(BF16) for a in range(hpb)]

            accs = [None] * hpb
            p_next = probs(0)
            for j in range(nk):
                p_cur = p_next
                if j + 1 < nk:
                    p_next = probs(j + 1)
                s, off = chunks[j]
                vblk = v1T_refs[s][0, 0, :, off:off + kc]
                for a in range(hpb):
                    pv = _dot(vblk, p_cur[a])
                    accs[a] = pv if j == 0 else accs[a] + pv
            for a in range(hpb):
                oT_scr[gs[a]] = accs[a][0:HEAD_DIM] / accs[a][HEAD_DIM:HEAD_DIM + 1]
            return carry

        lax.fori_loop(0, ATT_GROUP // hpb, heads, 0)

    @pl.when(jnp.logical_not(fast))
    def _():
        def head(g, carry):
            qT = qT_ref[0, g]
            m_scr[...] = jnp.full(m_scr.shape, -jnp.inf, F32)
            acc_scr[...] = jnp.zeros(acc_scr.shape, F32)

            for k1_ref, v1T_ref, n in zip(k1_refs, v1T_refs, seg_keys):
                def chunk(j, c2, k1_ref=k1_ref, v1T_ref=v1T_ref):
                    ks = pl.multiple_of(j * kc, kc)
                    sT = _dot(k1_ref[0, 0, pl.ds(ks, kc), 0:HEAD_DIM], qT)
                    m_old = m_scr[...]
                    m_new = jnp.maximum(m_old, jnp.max(sT, axis=0, keepdims=True))
                    p = jnp.exp2(sT - m_new).astype(BF16)
                    acc_scr[...] = (jnp.exp2(m_old - m_new) * acc_scr[...]
                                    + _dot(v1T_ref[0, 0, :, pl.ds(ks, kc)], p))
                    m_scr[...] = m_new
                    return c2

                lax.fori_loop(0, n // kc, chunk, 0)
            acc = acc_scr[...]
            oT_scr[g] = acc[0:HEAD_DIM] / acc[HEAD_DIM:HEAD_DIM + 1]
            return carry

        lax.fori_loop(0, ATT_GROUP, head, 0)

    o_ref[0] = oT_scr[...].reshape(ATT_GROUP * HEAD_DIM, tq).T.astype(BF16)


def _attention(qT, k1_segs, v1T_segs, *, tq, kc, hpb):
    B, _, _, T = qT.shape
    seg_keys = tuple(k1.shape[2] for k1 in k1_segs)
    return pl.pallas_call(
        functools.partial(_attn_kernel, seg_keys=seg_keys, kc=kc, tq=tq, hpb=hpb),
        grid=(B, ATT_KV_HEADS, T // tq),
        in_specs=(
            [pl.BlockSpec((1, ATT_GROUP, HEAD_DIM, tq), lambda b, h, i: (b, h, 0, i))]
            + [pl.BlockSpec((1, 1, n, LANES), lambda b, h, i: (b, h, 0, 0)) for n in seg_keys]
            + [pl.BlockSpec((1, 1, V_ROWS, n), lambda b, h, i: (b, h, 0, 0)) for n in seg_keys]),
        out_specs=pl.BlockSpec((1, tq, ATT_GROUP * HEAD_DIM), lambda b, h, i: (b, i, h)),
        out_shape=jax.ShapeDtypeStruct((B, T, ATT_WIDTH), BF16),
        scratch_shapes=[
            pltpu.VMEM((ATT_GROUP, HEAD_DIM, tq), F32),
            pltpu.VMEM((ATT_GROUP, LANES, tq), BF16),
            pltpu.VMEM((F32_SUBLANES, LANES), F32),
            pltpu.VMEM((V_ROWS, tq), F32),
            pltpu.VMEM((1, tq), F32),
        ],
        compiler_params=pltpu.CompilerParams(
            dimension_semantics=("parallel", "parallel", "arbitrary"), vmem_limit_bytes=VMEM_LIMIT),
        name="attention",
    )(qT, *k1_segs, *v1T_segs)


def _mlstm_kernel(qTf_ref, kf_ref, vTf_ref, qTb_ref, kb_ref, vTb_ref, grf_ref, grb_ref,
                  s0_ref, m0_ref, hf_ref, hb_ref, sout_ref, mout_ref, s_scr, m_scr, *, nc, bb):
    c = pl.program_id(1)
    n_state = 2 * M_HEADS

    @pl.when(c == 0)
    def _():
        for bi in range(bb):
            s_scr[bi * n_state:(bi + 1) * n_state] = s0_ref[bi]
            m_scr[bi * n_state:(bi + 1) * n_state] = m0_ref[bi]

    si = lax.broadcasted_iota(jnp.int32, (CHUNK, CHUNK), 0)
    ti = lax.broadcasted_iota(jnp.int32, (CHUNK, CHUNK), 1)
    ones_v = jnp.ones((M_DIM, CHUNK), BF16)
    zrows = jnp.zeros((CHUNK - M_HEADS, CHUNK), F32)

    dirs = [(bi, d) for bi in range(bb) for d in range(2)]
    valids = [(si <= ti), (si >= ti)]
    gates = []
    for bi, d in dirs:
        gr = (grf_ref if d == 0 else grb_ref)[bi]
        tri = valids[d].astype(BF16)
        gr_hi = gr.astype(BF16)
        gr_lo = (gr - gr_hi.astype(F32)).astype(BF16)
        gates.append((gr, _dot(gr_hi, tri) + _dot(gr_lo, tri)))

    pre = []
    for bi, d in dirs:
        qT_ref, k_ref, vT_ref = (qTf_ref, kf_ref, vTf_ref) if d == 0 else (qTb_ref, kb_ref, vTb_ref)
        for hd in range(M_HEADS):
            qT = qT_ref[bi, hd]
            kh = k_ref[bi, hd]
            v1T = jnp.concatenate([vT_ref[bi, hd], ones_v], axis=0)
            pre.append(dict(qT=qT, kh=kh, v1T=v1T, kq=_dot(kh, qT)))

    for (bi, d), (gr, arow_all) in zip(dirs, gates):
        g0 = 2 * M_HEADS * d
        a_rows = arow_all[g0 + M_HEADS:g0 + 2 * M_HEADS, :]
        r_rows = gr[g0:g0 + M_HEADS, :] - a_rows
        r_cols = jnp.concatenate([r_rows, zrows], axis=0).T
        last = CHUNK - 1 if d == 0 else 0
        for hd in range(M_HEADS):
            e = pre[(2 * bi + d) * M_HEADS + hd]
            a_row = a_rows[hd:hd + 1, :]
            x = jnp.where(valids[d], r_cols[:, hd:hd + 1], -jnp.inf)
            R_row = jnp.max(x, axis=0, keepdims=True)
            R_L = jnp.broadcast_to(R_row[:, last:last + 1], (1, LANES))
            wk = jnp.exp(r_rows[hd:hd + 1, :] - R_L)
            e.update(a_row=a_row, R_row=R_row, R_L=R_L, a_L=a_row[:, last:last + 1],
                     kqd=e["kq"] * jnp.exp(x - R_row),
                     uT=_dot((e["v1T"].astype(F32) * wk).astype(BF16), e["kh"]))

    outs = []
    for j, e in enumerate(pre):
        m = m_scr[j]
        st = s_scr[j]
        M_row = jnp.maximum(m, e["R_row"])
        s0T = (e["kqd"] * jnp.exp(e["R_row"] - M_row)).astype(BF16)
        qs = (e["qT"].astype(F32) * jnp.exp(m - M_row)).astype(BF16)
        lhs = jnp.concatenate([e["v1T"], st.astype(BF16)], axis=1)
        tot = _dot(lhs, jnp.concatenate([s0T, qs], axis=0))
        den = jnp.maximum(jnp.abs(tot[M_DIM:2 * M_DIM]), jnp.exp(-(e["a_row"] + M_row)))
        outs.append(tot[0:M_DIM] / den)
        M_L = jnp.maximum(m, e["R_L"])
        s_scr[j] = jnp.exp(m - M_L)[:, 0:M_DIM] * st + jnp.exp(e["R_L"] - M_L)[:, 0:M_DIM] * e["uT"]
        m_scr[j] = jnp.broadcast_to(e["a_L"], (1, LANES)) + M_L
    for bi in range(bb):
        o = bi * n_state
        hf_ref[bi] = jnp.concatenate(outs[o:o + M_HEADS], axis=0).T
        hb_ref[bi] = jnp.concatenate(outs[o + M_HEADS:o + n_state], axis=0).T

    @pl.when(c == nc - 1)
    def _():
        for bi in range(bb):
            sout_ref[bi] = s_scr[bi * n_state:(bi + 1) * n_state]
            mout_ref[bi] = m_scr[bi * n_state:(bi + 1) * n_state]


def _mlstm(qmT, km, vmT, grow, s0, m0):
    B, _, _, T = qmT.shape
    nc = T // CHUNK
    fwd3 = lambda b, c: (b, 0, c, 0)
    bwd3 = lambda b, c: (b, 0, nc - 1 - c, 0)
    fwdT = lambda b, c: (b, 0, 0, c)
    bwdT = lambda b, c: (b, 0, 0, nc - 1 - c)
    bb = MLSTM_BATCH
    rspec = lambda im: pl.BlockSpec((bb, M_HEADS, CHUNK, M_DIM), im)
    tspec = lambda im: pl.BlockSpec((bb, M_HEADS, M_DIM, CHUNK), im)
    n_state = 2 * M_HEADS
    return pl.pallas_call(
        functools.partial(_mlstm_kernel, nc=nc, bb=bb),
        grid=(B // bb, nc),
        in_specs=[
            tspec(fwdT), rspec(fwd3), tspec(fwdT),
            tspec(bwdT), rspec(bwd3), tspec(bwdT),
            pl.BlockSpec((bb, N_GATES, CHUNK), lambda b, c: (b, 0, c)),
            pl.BlockSpec((bb, N_GATES, CHUNK), lambda b, c: (b, 0, nc - 1 - c)),
            pl.BlockSpec((bb, n_state, 2 * M_DIM, M_DIM), lambda b, c: (b, 0, 0, 0)),
            pl.BlockSpec((bb, n_state, 1, LANES), lambda b, c: (b, 0, 0, 0)),
        ],
        out_specs=[
            pl.BlockSpec((bb, CHUNK, M_WIDTH), lambda b, c: (b, c, 0)),
            pl.BlockSpec((bb, CHUNK, M_WIDTH), lambda b, c: (b, nc - 1 - c, 0)),
            pl.BlockSpec((bb, n_state, 2 * M_DIM, M_DIM), lambda b, c: (b, 0, 0, 0)),
            pl.BlockSpec((bb, n_state, 1, LANES), lambda b, c: (b, 0, 0, 0)),
        ],
        out_shape=[
            jax.ShapeDtypeStruct((B, T, M_WIDTH), F32),
            jax.ShapeDtypeStruct((B, T, M_WIDTH), F32),
            jax.ShapeDtypeStruct((B, n_state, 2 * M_DIM, M_DIM), F32),
            jax.ShapeDtypeStruct((B, n_state, 1, LANES), F32),
        ],
        scratch_shapes=[pltpu.VMEM((bb * n_state, 2 * M_DIM, M_DIM), F32),
                        pltpu.VMEM((bb * n_state, 1, LANES), F32)],
        compiler_params=pltpu.CompilerParams(
            dimension_semantics=("parallel", "arbitrary"), vmem_limit_bytes=VMEM_LIMIT),
        name="mlstm",
    )(qmT, km, vmT, qmT, km, vmT, grow, grow, s0, m0)


def _post_kernel(att_ref, hf_ref, hb_ref, so_ref, go_ref, x_ref, mod_ref, wo_ref, l1g_ref, l1b_ref,
                 w1_ref, w2_ref, l2g_ref, l2b_ref, o_ref, *, ff_chunk):
    mod = mod_ref[0]
    g1 = mod[:, 2 * D_MODEL:3 * D_MODEL]
    sh2 = mod[:, 3 * D_MODEL:4 * D_MODEL]
    sc2 = mod[:, 4 * D_MODEL:5 * D_MODEL]
    g2 = mod[:, 5 * D_MODEL:6 * D_MODEL]
    half = att_ref.shape[1] // 2
    rows = [slice(r * half, (r + 1) * half) for r in range(2)]
    ys = []
    for r in rows:
        m_out = (so_ref[0, r].astype(F32) * (hf_ref[0, r] + hb_ref[0, r])).astype(BF16)
        cat = jnp.concatenate([att_ref[0, r], m_out, go_ref[0, r]], axis=1)
        ys.append(_dot(cat, wo_ref[...]))
    for r, y in zip(rows, ys):
        x1 = _layer_norm(DEEPNORM_ALPHA * x_ref[0, r] + g1 * y, l1g_ref[...], l1b_ref[...])
        h2 = (x1 * (1.0 + sc2) + sh2).astype(BF16)
        f = None
        for j in range(D_FF // ff_chunk):
            a = jnp.maximum(_dot(h2, w1_ref[:, j * ff_chunk:(j + 1) * ff_chunk]), 0.0)
            part = _dot((a * a).astype(BF16), w2_ref[j * ff_chunk:(j + 1) * ff_chunk, :])
            f = part if f is None else f + part
        o_ref[0, r] = _layer_norm(DEEPNORM_ALPHA * x1 + g2 * f, l2g_ref[...], l2b_ref[...])


def _post(att, hf, hb, so, go, x, mod, mod_row, wo, l1g, l1b, w1, w2, l2g, l2b, *, tm, ff_chunk):
    B, T, _ = x.shape
    mod_idx = (lambda b, i: (b + mod_row, 0, 0)) if mod_row else (lambda b, i: (0, 0, 0))
    tok = lambda w: pl.BlockSpec((1, tm, w), lambda b, i: (b, i, 0))
    return pl.pallas_call(
        functools.partial(_post_kernel, ff_chunk=ff_chunk),
        grid=(B, T // tm),
        in_specs=[
            tok(ATT_WIDTH), tok(M_WIDTH), tok(M_WIDTH), tok(M_WIDTH), tok(G_WIDTH), tok(D_MODEL),
            pl.BlockSpec((1, 1, 6 * D_MODEL), mod_idx),
            _resident((D_MODEL, D_MODEL)), _resident((1, D_MODEL)), _resident((1, D_MODEL)),
            _resident((D_MODEL, D_FF)), _resident((D_FF, D_MODEL)),
            _resident((1, D_MODEL)), _resident((1, D_MODEL)),
        ],
        out_specs=tok(D_MODEL),
        out_shape=jax.ShapeDtypeStruct((B, T, D_MODEL), F32),
        compiler_params=pltpu.CompilerParams(
            dimension_semantics=("parallel", "parallel"), vmem_limit_bytes=VMEM_LIMIT),
        name="post",
    )(att, hf, hb, so, go, x, mod, wo, l1g, l1b, w1, w2, l2g, l2b)


def _rope_tables(T):
    rows = T // GRID_W
    r, cidx = jnp.meshgrid(jnp.arange(rows, dtype=F32), jnp.arange(GRID_W, dtype=F32), indexing='ij')
    n_freq = HEAD_DIM // 4
    inv = ROPE_BASE ** (-jnp.arange(n_freq, dtype=F32) / n_freq)
    ang_r = (r.reshape(-1)[:, None] * inv[None, :]).T
    ang_c = (cidx.reshape(-1)[:, None] * inv[None, :]).T
    cos = jnp.concatenate([jnp.cos(ang_r)] * 2 + [jnp.cos(ang_c)] * 2, axis=0)
    sin = jnp.concatenate([-jnp.sin(ang_r), jnp.sin(ang_r), -jnp.sin(ang_c), jnp.sin(ang_c)], axis=0)
    return cos, sin


def _with_ones_rows(vT):
    ones = jnp.ones(vT.shape[:-2] + (BF16_SUBLANES, vT.shape[-1]), vT.dtype)
    return jnp.concatenate([vT, ones], axis=-2)


def _pack_state(C, n, m):
    B = C.shape[0]
    ns = 2 * M_HEADS
    CT = jnp.swapaxes(C.reshape(B, ns, M_DIM, M_DIM), -1, -2).astype(F32)
    n = jnp.broadcast_to(n.reshape(B, ns, 1, M_DIM).astype(F32), (B, ns, M_DIM, M_DIM))
    m = jnp.broadcast_to(m.reshape(B, ns, 1, 1).astype(F32), (B, ns, 1, LANES))
    return jnp.concatenate([CT, n], axis=-2), m


def kernel(x_prompt, x_sample, c, cache_attn_k, cache_attn_v, state_mlstm_C, state_mlstm_n, state_mlstm_m,
           c_ctx, w_mod, b_mod, w_in, q_norm_g, k_norm_g, mlstm_gate_b, gmlp_ln_g, gmlp_ln_b, gmlp_ws,
           gmlp_bs, w_out, ln1_g, ln1_b, w_ff1, w_ff2, ln2_g, ln2_b):
    BP, TP, _ = x_prompt.shape
    BS, TS, _ = x_sample.shape
    n_mod = MOD_ROWS
    assert BS + 1 <= n_mod
    cvec = jnp.zeros((n_mod, D_MODEL), F32).at[0].set(c_ctx).at[1:1 + BS].set(c)
    mod_all = _modulation(cvec, w_mod, b_mod).reshape(DEPTH, n_mod, 1, 6 * D_MODEL)

    cos_s, sin_s = _rope_tables(TS)
    tm_p, tm_s = TP, INPROJ_TM

    xp, xs = x_prompt, x_sample
    ks_, vs_, Cs_, ns_, ms_ = [], [], [], [], []
    for l in range(DEPTH):
        wl = w_in[l]
        g0 = OFF_OM + M_WIDTH
        w_in_l = jnp.concatenate(
            [wl[:, :g0], wl[:, g0 + N_GATES:], wl[:, g0:g0 + N_GATES],
             jnp.zeros((D_MODEL, GATE_PAD - N_GATES), F32)], axis=1).astype(BF16)
        gate_b = jnp.zeros((1, GATE_PAD), F32).at[0, :N_GATES].set(mlstm_gate_b[l].reshape(-1))
        lng = gmlp_ln_g[l].reshape(1, G_WIDTH)
        lnb = gmlp_ln_b[l].reshape(1, G_WIDTH)
        ws = gmlp_ws[l].astype(BF16)
        bsx = jnp.repeat(gmlp_bs[l].T, G_CH, axis=1)
        wo = w_out[l].astype(BF16)
        w1 = w_ff1[l].astype(BF16)
        w2 = w_ff2[l].astype(BF16)
        l1g, l1b = ln1_g[l].reshape(1, -1), ln1_b[l].reshape(1, -1)
        l2g, l2b = ln2_g[l].reshape(1, -1), ln2_b[l].reshape(1, -1)
        mod = mod_all[l]

        def gains(tm):
            gq = jnp.broadcast_to((q_norm_g[l] * (HEAD_DIM ** -0.5 * LOG2_E))[:, None], (HEAD_DIM, tm))
            gk = jnp.broadcast_to(k_norm_g[l][:, None], (HEAD_DIM, tm))
            return gq, gk

        gq, gk = gains(tm_p)
        (qT, k, v1T, qmT, km, vmT, so, go, grow, k32, v32) = _inproj(
            xp, mod, 0, w_in_l, gq, gk, None, None, gate_b, lng, lnb, ws, bsx,
            rope=False, emit_kv32=True, tm=tm_p)
        att = _attention(qT, [k], [v1T], tq=TP, kc=TP, hpb=ATT_GROUP)
        zero_state = _pack_state(jnp.zeros((BP, 2, M_HEADS, M_DIM, M_DIM), F32),
                                 jnp.zeros((BP, 2, M_HEADS, M_DIM), F32),
                                 jnp.zeros((BP, 2, M_HEADS), F32))
        hf, hb, s_fin, m_fin = _mlstm(qmT, km, vmT, grow, *zero_state)
        fold = lambda a: a.reshape(BP * TP // POST_TM, POST_TM, a.shape[-1])
        xp = _post(fold(att), fold(hf), fold(hb), fold(so), fold(go), fold(xp), mod, 0, wo, l1g, l1b, w1, w2,
                   l2g, l2b, tm=POST_TM, ff_chunk=1024).reshape(BP, TP, D_MODEL)
        ks_.append(k32.reshape(BP, TP, ATT_KV_HEADS, HEAD_DIM))
        vs_.append(v32.reshape(BP, TP, ATT_KV_HEADS, HEAD_DIM))
        Cs_.append(jnp.swapaxes(s_fin[:, :, :M_DIM, :], -1, -2).reshape(BP, 2, M_HEADS, M_DIM, M_DIM))
        ns_.append(s_fin[:, :, M_DIM, :].reshape(BP, 2, M_HEADS, M_DIM))
        ms_.append(m_fin[:, :, 0, 0].reshape(BP, 2, M_HEADS))

        gq, gk = gains(tm_s)
        (qT, k, v1T, qmT, km, vmT, so, go, grow) = _inproj(
            xs, mod, 1, w_in_l, gq, gk, cos_s, sin_s, gate_b, lng, lnb, ws, bsx,
            rope=True, emit_kv32=False, tm=tm_s)
        k_c = jnp.transpose(cache_attn_k[:, l], (0, 2, 1, 3)).astype(BF16)
        ones_col = jnp.zeros(k_c.shape[:-1] + (LANES - HEAD_DIM,), BF16).at[..., 0].set(1.0)
        k_c = jnp.concatenate([k_c, ones_col], axis=-1)
        vT_c = _with_ones_rows(jnp.transpose(cache_attn_v[:, l], (0, 2, 3, 1)).astype(BF16))
        att = _attention(qT, [k_c, k], [vT_c, v1T], tq=ATT_TQ, kc=ATT_KC, hpb=ATT_HPB)
        state = _pack_state(state_mlstm_C[:, l], state_mlstm_n[:, l], state_mlstm_m[:, l])
        hf, hb, _, _ = _mlstm(qmT, km, vmT, grow, *state)
        xs = _post(att, hf, hb, so, go, xs, mod, 1, wo, l1g, l1b, w1, w2, l2g, l2b, tm=POST_TM, ff_chunk=1024)

    return (xp, xs, jnp.stack(ks_, axis=1), jnp.stack(vs_, axis=1), jnp.stack(Cs_, axis=1),
            jnp.stack(ns_, axis=1), jnp.stack(ms_, axis=1))
```

```python
import functools

import jax
import jax.numpy as jnp
from jax import lax
from jax.experimental import pallas as pl
from jax.experimental.pallas import tpu as pltpu

D_MODEL = 1024
DEPTH = 2
GRID_W = 64
HEAD_DIM = 64
ATT_HEADS = 8
ATT_KV_HEADS = 2
ATT_GROUP = ATT_HEADS // ATT_KV_HEADS
ATT_WIDTH = ATT_HEADS * HEAD_DIM
KV_WIDTH = ATT_KV_HEADS * HEAD_DIM
M_HEADS = 4
M_DIM = 64
M_WIDTH = M_HEADS * M_DIM
G_GROUPS = 4
G_WIDTH = D_MODEL - ATT_WIDTH - M_WIDTH
G_CH = G_WIDTH // G_GROUPS
CHUNK = 128
D_FF = 4 * D_MODEL
ROPE_BASE = 10000.0
EPS = 1e-6
DEEPNORM_ALPHA = (2 * DEPTH) ** 0.25
N_GATES = 4 * M_HEADS

LANES = 128
F32_SUBLANES = 8
BF16_SUBLANES = 16
GATE_PAD = LANES
OFF_QA = 0
OFF_KA = OFF_QA + ATT_WIDTH
OFF_VA = OFF_KA + KV_WIDTH
OFF_QM = OFF_VA + KV_WIDTH
OFF_KM = OFF_QM + M_WIDTH
OFF_VM = OFF_KM + M_WIDTH
OFF_OM = OFF_VM + M_WIDTH
OFF_UG = OFF_OM + M_WIDTH
OFF_VG = OFF_UG + G_WIDTH
OFF_GM = OFF_VG + G_WIDTH
PROJ_PAD = OFF_GM + GATE_PAD
V_ROWS = HEAD_DIM + BF16_SUBLANES
VMEM_LIMIT = 56 * 1024 * 1024

ATT_TQ, ATT_KC, ATT_HPB = 2048, 256, 2
ATT_CT = 512
INPROJ_TM = 1024
POST_TM = 1024
MOD_BN = 1536
MOD_ROWS = 16
MLSTM_BATCH = 4
LOG2_E = 1.4426950408889634
BOUND_MARGIN = 1.0 + 2.0 ** -6
MAX_FAST_SHIFT = 60.0

F32 = jnp.float32
BF16 = jnp.bfloat16


def _dot(a, b):
    return jnp.dot(a, b, preferred_element_type=F32)


def _resident(shape):
    nd = len(shape)
    return pl.BlockSpec(shape, lambda *_: (0,) * nd, pipeline_mode=pl.Buffered(1))


def _sigmoid(x):
    return 1.0 / (1.0 + jnp.exp(-x))


def _log_sigmoid(x):
    return jnp.minimum(x, 0.0) - jnp.log1p(jnp.exp(-jnp.abs(x)))


def _layer_norm(x, g, b):
    mu = jnp.mean(x, axis=-1, keepdims=True)
    xc = x - mu
    var = jnp.mean(xc * xc, axis=-1, keepdims=True)
    return xc * lax.rsqrt(var + EPS) * g + b


def _mod_kernel(c_ref, w_ref, b_ref, o_ref):
    c = c_ref[...]
    s = (c * _sigmoid(c)).astype(BF16)
    o_ref[0] = _dot(s, w_ref[0].astype(BF16)) + b_ref[0]


def _modulation(cvec, w_mod, b_mod):
    n = cvec.shape[0]
    bn = MOD_BN
    return pl.pallas_call(
        _mod_kernel,
        grid=(DEPTH, 6 * D_MODEL // bn),
        in_specs=[
            pl.BlockSpec((n, D_MODEL), lambda l, j: (0, 0)),
            pl.BlockSpec((1, D_MODEL, bn), lambda l, j: (l, 0, j)),
            pl.BlockSpec((1, 1, bn), lambda l, j: (l, 0, j)),
        ],
        out_specs=pl.BlockSpec((1, n, bn), lambda l, j: (l, 0, j)),
        out_shape=jax.ShapeDtypeStruct((DEPTH, n, 6 * D_MODEL), F32),
        compiler_params=pltpu.CompilerParams(
            dimension_semantics=("arbitrary", "arbitrary"), vmem_limit_bytes=VMEM_LIMIT),
        name="modulation",
    )(cvec, w_mod, b_mod.reshape(DEPTH, 1, 6 * D_MODEL))


def _norm_rope_heads(zT, gain, cos, sin, n_heads):
    outs = []
    for hd in range(n_heads):
        t = zT[hd * HEAD_DIM:(hd + 1) * HEAD_DIM, :]
        ms = jnp.mean(t * t, axis=0, keepdims=True)
        y = t * lax.rsqrt(ms + EPS) * gain
        if cos is not None:
            q = HEAD_DIM // 4
            sw = jnp.concatenate([y[q:2 * q], y[0:q], y[3 * q:4 * q], y[2 * q:3 * q]], axis=0)
            y = y * cos + sw * sin
        outs.append(y)
    return outs


def _inproj_kernel(*refs, rope, emit_kv32, tm):
    it = iter(refs)
    x_ref, mod_ref, w_ref, gq_ref, gk_ref = (next(it) for _ in range(5))
    cos_ref, sin_ref = (next(it), next(it)) if rope else (None, None)
    gb_ref, lng_ref, lnb_ref, ws_ref, bsx_ref = (next(it) for _ in range(5))
    qT_ref, k_ref, vT_ref, qmT_ref, km_ref, vmT_ref, so_ref, go_ref, grow_ref = (
        next(it) for _ in range(9))
    k32_ref, v32_ref = (next(it), next(it)) if emit_kv32 else (None, None)

    x = x_ref[0]
    mod = mod_ref[0]
    sh1 = mod[:, 0:D_MODEL]
    sc1 = mod[:, D_MODEL:2 * D_MODEL]
    h = (x * (1.0 + sc1) + sh1).astype(BF16)

    cos = cos_ref[...] if rope else None
    sin = sin_ref[...] if rope else None

    zvg = _dot(h, w_ref[:, OFF_VG:OFF_VG + G_WIDTH])
    zu = _dot(h, w_ref[:, OFF_UG:OFF_UG + G_WIDTH])
    vn = _layer_norm(zvg, lng_ref[...], lnb_ref[...])

    zq = _dot(h, w_ref[:, OFF_QA:OFF_QA + ATT_WIDTH])
    q_heads = _norm_rope_heads(zq.T, gq_ref[...], cos, sin, ATT_HEADS)
    for hd in range(ATT_HEADS):
        qT_ref[0, hd] = q_heads[hd].astype(BF16)

    zk = _dot(h, w_ref[:, OFF_KA:OFF_KA + KV_WIDTH])
    k_heads = _norm_rope_heads(zk.T, gk_ref[...], cos, sin, ATT_KV_HEADS)
    kk = jnp.concatenate(k_heads, axis=0).T
    ones_col = (lax.broadcasted_iota(jnp.int32, (tm, LANES - HEAD_DIM), 1) == 0).astype(F32)
    for hd in range(ATT_KV_HEADS):
        k_ref[0, hd] = jnp.concatenate(
            [kk[:, hd * HEAD_DIM:(hd + 1) * HEAD_DIM], ones_col], axis=1).astype(BF16)
    zv = _dot(h, w_ref[:, OFF_VA:OFF_VA + KV_WIDTH])
    vT = zv.T
    ones_rows = jnp.ones((BF16_SUBLANES, tm), BF16)
    for hd in range(ATT_KV_HEADS):
        vT_ref[0, hd] = jnp.concatenate(
            [vT[hd * HEAD_DIM:(hd + 1) * HEAD_DIM].astype(BF16), ones_rows], axis=0)
    if emit_kv32:
        k32_ref[0] = kk
        v32_ref[0] = zv

    zqmT = _dot(h, w_ref[:, OFF_QM:OFF_QM + M_WIDTH]).T
    zkm = _dot(h, w_ref[:, OFF_KM:OFF_KM + M_WIDTH]) * (M_DIM ** -0.5)
    zvmT = _dot(h, w_ref[:, OFF_VM:OFF_VM + M_WIDTH]).T
    for hd in range(M_HEADS):
        qmT_ref[0, hd] = zqmT[hd * M_DIM:(hd + 1) * M_DIM, :].astype(BF16)
        km_ref[0, hd] = zkm[:, hd * M_DIM:(hd + 1) * M_DIM].astype(BF16)
        vmT_ref[0, hd] = zvmT[hd * M_DIM:(hd + 1) * M_DIM, :].astype(BF16)
    zo = _dot(h, w_ref[:, OFF_OM:OFF_OM + M_WIDTH])
    so_ref[0] = _sigmoid(zo).astype(BF16)

    zg = _dot(h, w_ref[:, OFF_GM:OFF_GM + GATE_PAD]) + gb_ref[...]
    lane = lax.broadcasted_iota(jnp.int32, zg.shape, 1)
    is_forget = (lane % (2 * M_HEADS)) >= M_HEADS
    grow_ref[0] = jnp.where(is_forget, _log_sigmoid(zg), zg).T[0:N_GATES, :]

    group = lax.broadcasted_iota(jnp.int32, (CHUNK, G_WIDTH), 1) // G_CH
    bsx = bsx_ref[...]
    s_chunks = []
    for j in range(tm // CHUNK):
        vc = vn[j * CHUNK:(j + 1) * CHUNK, :]
        acc = bsx
        for g in range(G_GROUPS):
            acc = acc + _dot(ws_ref[g], jnp.where(group == g, vc, 0.0).astype(BF16))
        s_chunks.append(acc)
    s = jnp.concatenate(s_chunks, axis=0) if len(s_chunks) > 1 else s_chunks[0]
    go_ref[0] = (zu * s).astype(BF16)


def _inproj(x, mod, mod_row, w_in, gq, gk, cos, sin, gate_b, lng, lnb, ws, bsx, *, rope, emit_kv32, tm):
    B, T, _ = x.shape
    nt = T // tm
    mod_idx = (lambda b, i: (b + mod_row, 0, 0)) if mod_row else (lambda b, i: (0, 0, 0))
    in_specs = [
        pl.BlockSpec((1, tm, D_MODEL), lambda b, i: (b, i, 0)),
        pl.BlockSpec((1, 1, 6 * D_MODEL), mod_idx),
        _resident((D_MODEL, PROJ_PAD)),
        _resident((HEAD_DIM, tm)),
        _resident((HEAD_DIM, tm)),
    ]
    args = [x, mod, w_in, gq, gk]
    if rope:
        in_specs += [pl.BlockSpec((HEAD_DIM, tm), lambda b, i: (0, i))] * 2
        args += [cos, sin]
    in_specs += [
        _resident((1, GATE_PAD)),
        _resident((1, G_WIDTH)),
        _resident((1, G_WIDTH)),
        _resident((G_GROUPS, CHUNK, CHUNK)),
        _resident((CHUNK, G_WIDTH)),
    ]
    args += [gate_b, lng, lnb, ws, bsx]
    out_shape = [
        jax.ShapeDtypeStruct((B, ATT_HEADS, HEAD_DIM, T), BF16),
        jax.ShapeDtypeStruct((B, ATT_KV_HEADS, T, LANES), BF16),
        jax.ShapeDtypeStruct((B, ATT_KV_HEADS, V_ROWS, T), BF16),
        jax.ShapeDtypeStruct((B, M_HEADS, M_DIM, T), BF16),
        jax.ShapeDtypeStruct((B, M_HEADS, T, M_DIM), BF16),
        jax.ShapeDtypeStruct((B, M_HEADS, M_DIM, T), BF16),
        jax.ShapeDtypeStruct((B, T, M_WIDTH), BF16),
        jax.ShapeDtypeStruct((B, T, G_WIDTH), BF16),
        jax.ShapeDtypeStruct((B, N_GATES, T), F32),
    ]
    out_specs = [
        pl.BlockSpec((1, ATT_HEADS, HEAD_DIM, tm), lambda b, i: (b, 0, 0, i)),
        pl.BlockSpec((1, ATT_KV_HEADS, tm, LANES), lambda b, i: (b, 0, i, 0)),
        pl.BlockSpec((1, ATT_KV_HEADS, V_ROWS, tm), lambda b, i: (b, 0, 0, i)),
        pl.BlockSpec((1, M_HEADS, M_DIM, tm), lambda b, i: (b, 0, 0, i)),
        pl.BlockSpec((1, M_HEADS, tm, M_DIM), lambda b, i: (b, 0, i, 0)),
        pl.BlockSpec((1, M_HEADS, M_DIM, tm), lambda b, i: (b, 0, 0, i)),
        pl.BlockSpec((1, tm, M_WIDTH), lambda b, i: (b, i, 0)),
        pl.BlockSpec((1, tm, G_WIDTH), lambda b, i: (b, i, 0)),
        pl.BlockSpec((1, N_GATES, tm), lambda b, i: (b, 0, i)),
    ]
    if emit_kv32:
        out_shape += [jax.ShapeDtypeStruct((B, T, KV_WIDTH), F32)] * 2
        out_specs += [pl.BlockSpec((1, tm, KV_WIDTH), lambda b, i: (b, i, 0))] * 2
    return pl.pallas_call(
        functools.partial(_inproj_kernel, rope=rope, emit_kv32=emit_kv32, tm=tm),
        grid=(B, nt),
        in_specs=in_specs,
        out_specs=out_specs,
        out_shape=out_shape,
        compiler_params=pltpu.CompilerParams(
            dimension_semantics=("parallel", "parallel"), vmem_limit_bytes=VMEM_LIMIT),
        name="inproj",
    )(*args)


def _attn_kernel(qT_ref, *refs, seg_keys, kc, tq, hpb, ct_w):
    n_seg = len(seg_keys)
    k1_refs, v1T_refs = refs[:n_seg], refs[n_seg:2 * n_seg]
    o_ref, oT_scr, q1_scr, kmax_scr, acc_scr, m_scr = refs[2 * n_seg:]
    chunks = [(s, off) for s, n in enumerate(seg_keys) for off in range(0, n, kc)]
    nk = len(chunks)
    i = pl.program_id(2)

    @pl.when(i == 0)
    def _():
        kmax2 = None
        for k1_ref in k1_refs:
            kf = k1_ref[0, 0].astype(F32)
            n2 = jnp.max(jnp.sum(kf * kf, axis=1, keepdims=True) - 1.0, axis=0, keepdims=True)
            kmax2 = n2 if kmax2 is None else jnp.maximum(kmax2, n2)
        kmax_scr[...] = jnp.broadcast_to(kmax2, kmax_scr.shape)

    kmax2 = kmax_scr[0:1, 0:1]
    row0 = lax.broadcasted_iota(jnp.int32, (BF16_SUBLANES, tq), 0) == 0
    pad = jnp.zeros((LANES - HEAD_DIM - BF16_SUBLANES, tq), BF16)
    cmax = None
    for g in range(ATT_GROUP):
        qT = qT_ref[0, g]
        qf = qT.astype(F32)
        c = jnp.sqrt(jnp.sum(qf * qf, axis=0, keepdims=True) * kmax2) * BOUND_MARGIN
        shift = jnp.where(row0, -c, 0.0).astype(BF16)
        q1_scr[g] = jnp.concatenate([qT, shift, pad], axis=0)
        cg = jnp.max(c)
        cmax = cg if cmax is None else jnp.maximum(cmax, cg)
    fast = cmax <= MAX_FAST_SHIFT

    @pl.when(fast)
    def _():
        def heads(gi, carry):
            gs = [gi * hpb + a for a in range(hpb)]
            q1s = [q1_scr[g] for g in gs]

            n_ct = tq // ct_w
            cols = [slice(t * ct_w, (t + 1) * ct_w) for t in range(n_ct)]

            def prob(j, a, t):
                s, off = chunks[j]
                return jnp.exp2(_dot(k1_refs[s][0, 0, off:off + kc, :], q1s[a][:, cols[t]])).astype(BF16)

            accs = [[None] * n_ct for _ in range(hpb)]
            p_next = [[prob(0, a, t) for t in range(n_ct)] for a in range(hpb)]
            for j in range(nk):
                p_cur = p_next
                p_next = [[None] * n_ct for _ in range(hpb)]
                s, off = chunks[j]
                vblk = v1T_refs[s][0, 0, :, off:off + kc]
                for a in range(hpb):
                    for t in range(n_ct):
                        if j + 1 < nk:
                            p_next[a][t] = prob(j + 1, a, t)
                        pv = _dot(vblk, p_cur[a][t])
                        accs[a][t] = pv if j == 0 else accs[a][t] + pv
            for a in range(hpb):
                for t in range(n_ct):
                    acc = accs[a][t]
                    oT_scr[gs[a], :, cols[t]] = acc[0:HEAD_DIM] / acc[HEAD_DIM:HEAD_DIM + 1]
            return carry

        lax.fori_loop(0, ATT_GROUP // hpb, heads, 0)

    @pl.when(jnp.logical_not(fast))
    def _():
        def head(g, carry):
            qT = qT_ref[0, g]
            m_scr[...] = jnp.full(m_scr.shape, -jnp.inf, F32)
            acc_scr[...] = jnp.zeros(acc_scr.shape, F32)

            for k1_ref, v1T_ref, n in zip(k1_refs, v1T_refs, seg_keys):
                def chunk(j, c2, k1_ref=k1_ref, v1T_ref=v1T_ref):
                    ks = pl.multiple_of(j * kc, kc)
                    sT = _dot(k1_ref[0, 0, pl.ds(ks, kc), 0:HEAD_DIM], qT)
                    m_old = m_scr[...]
                    m_new = jnp.maximum(m_old, jnp.max(sT, axis=0, keepdims=True))
                    p = jnp.exp2(sT - m_new).astype(BF16)
                    acc_scr[...] = (jnp.exp2(m_old - m_new) * acc_scr[...]
                                    + _dot(v1T_ref[0, 0, :, pl.ds(ks, kc)], p))
                    m_scr[...] = m_new
                    return c2

                lax.fori_loop(0, n // kc, chunk, 0)
            acc = acc_scr[...]
            oT_scr[g] = acc[0:HEAD_DIM] / acc[HEAD_DIM:HEAD_DIM + 1]
            return carry

        lax.fori_loop(0, ATT_GROUP, head, 0)

    o_ref[0] = oT_scr[...].reshape(ATT_GROUP * HEAD_DIM, tq).T.astype(BF16)


def _attention(qT, k1_segs, v1T_segs, *, tq, kc, hpb):
    B, _, _, T = qT.shape
    seg_keys = tuple(k1.shape[2] for k1 in k1_segs)
    return pl.pallas_call(
        functools.partial(_attn_kernel, seg_keys=seg_keys, kc=kc, tq=tq, hpb=hpb, ct_w=min(tq, ATT_CT)),
        grid=(B, ATT_KV_HEADS, T // tq),
        in_specs=(
            [pl.BlockSpec((1, ATT_GROUP, HEAD_DIM, tq), lambda b, h, i: (b, h, 0, i))]
            + [pl.BlockSpec((1, 1, n, LANES), lambda b, h, i: (b, h, 0, 0)) for n in seg_keys]
            + [pl.BlockSpec((1, 1, V_ROWS, n), lambda b, h, i: (b, h, 0, 0)) for n in seg_keys]),
        out_specs=pl.BlockSpec((1, tq, ATT_GROUP * HEAD_DIM), lambda b, h, i: (b, i, h)),
        out_shape=jax.ShapeDtypeStruct((B, T, ATT_WIDTH), BF16),
        scratch_shapes=[
            pltpu.VMEM((ATT_GROUP, HEAD_DIM, tq), F32),
            pltpu.VMEM((ATT_GROUP, LANES, tq), BF16),
            pltpu.VMEM((F32_SUBLANES, LANES), F32),
            pltpu.VMEM((V_ROWS, tq), F32),
            pltpu.VMEM((1, tq), F32),
        ],
        compiler_params=pltpu.CompilerParams(
            dimension_semantics=("parallel", "parallel", "arbitrary"), vmem_limit_bytes=VMEM_LIMIT),
        name="attention",
    )(qT, *k1_segs, *v1T_segs)


def _mlstm_kernel(qTf_ref, kf_ref, vTf_ref, qTb_ref, kb_ref, vTb_ref, grf_ref, grb_ref,
                  s0_ref, m0_ref, hf_ref, hb_ref, sout_ref, mout_ref, s_scr, m_scr, *, nc, bb):
    c = pl.program_id(1)
    n_state = 2 * M_HEADS

    @pl.when(c == 0)
    def _():
        for bi in range(bb):
            s_scr[bi * n_state:(bi + 1) * n_state] = s0_ref[bi]
            m_scr[bi * n_state:(bi + 1) * n_state] = m0_ref[bi]

    si = lax.broadcasted_iota(jnp.int32, (CHUNK, CHUNK), 0)
    ti = lax.broadcasted_iota(jnp.int32, (CHUNK, CHUNK), 1)
    ones_v = jnp.ones((M_DIM, CHUNK), BF16)
    zrows = jnp.zeros((CHUNK - M_HEADS, CHUNK), F32)

    dirs = [(bi, d) for bi in range(bb) for d in range(2)]
    valids = [(si <= ti), (si >= ti)]
    gates = []
    for bi, d in dirs:
        gr = (grf_ref if d == 0 else grb_ref)[bi]
        tri = valids[d].astype(BF16)
        gr_hi = gr.astype(BF16)
        gr_lo = (gr - gr_hi.astype(F32)).astype(BF16)
        gates.append((gr, _dot(gr_hi, tri) + _dot(gr_lo, tri)))

    pre = []
    for bi, d in dirs:
        qT_ref, k_ref, vT_ref = (qTf_ref, kf_ref, vTf_ref) if d == 0 else (qTb_ref, kb_ref, vTb_ref)
        for hd in range(M_HEADS):
            qT = qT_ref[bi, hd]
            kh = k_ref[bi, hd]
            v1T = jnp.concatenate([vT_ref[bi, hd], ones_v], axis=0)
            pre.append(dict(qT=qT, kh=kh, v1T=v1T, kq=_dot(kh, qT)))

    for (bi, d), (gr, arow_all) in zip(dirs, gates):
        g0 = 2 * M_HEADS * d
        a_rows = arow_all[g0 + M_HEADS:g0 + 2 * M_HEADS, :]
        r_rows = gr[g0:g0 + M_HEADS, :] - a_rows
        r_cols = jnp.concatenate([r_rows, zrows], axis=0).T
        last = CHUNK - 1 if d == 0 else 0
        for hd in range(M_HEADS):
            e = pre[(2 * bi + d) * M_HEADS + hd]
            a_row = a_rows[hd:hd + 1, :]
            x = jnp.where(valids[d], r_cols[:, hd:hd + 1], -jnp.inf)
            R_row = jnp.max(x, axis=0, keepdims=True)
            R_L = jnp.broadcast_to(R_row[:, last:last + 1], (1, LANES))
            wk = jnp.exp(r_rows[hd:hd + 1, :] - R_L)
            e.update(a_row=a_row, R_row=R_row, R_L=R_L, a_L=a_row[:, last:last + 1],
                     kqd=e["kq"] * jnp.exp(x - R_row),
                     uT=_dot((e["v1T"].astype(F32) * wk).astype(BF16), e["kh"]))

    outs = []
    for j, e in enumerate(pre):
        m = m_scr[j]
        st = s_scr[j]
        M_row = jnp.maximum(m, e["R_row"])
        s0T = (e["kqd"] * jnp.exp(e["R_row"] - M_row)).astype(BF16)
        qs = (e["qT"].astype(F32) * jnp.exp(m - M_row)).astype(BF16)
        lhs = jnp.concatenate([e["v1T"], st.astype(BF16)], axis=1)
        tot = _dot(lhs, jnp.concatenate([s0T, qs], axis=0))
        den = jnp.maximum(jnp.abs(tot[M_DIM:2 * M_DIM]), jnp.exp(-(e["a_row"] + M_row)))
        outs.append(tot[0:M_DIM] / den)
        M_L = jnp.maximum(m, e["R_L"])
        s_scr[j] = jnp.exp(m - M_L)[:, 0:M_DIM] * st + jnp.exp(e["R_L"] - M_L)[:, 0:M_DIM] * e["uT"]
        m_scr[j] = jnp.broadcast_to(e["a_L"], (1, LANES)) + M_L
    for bi in range(bb):
        o = bi * n_state
        hf_ref[bi] = jnp.concatenate(outs[o:o + M_HEADS], axis=0).T
        hb_ref[bi] = jnp.concatenate(outs[o + M_HEADS:o + n_state], axis=0).T

    @pl.when(c == nc - 1)
    def _():
        for bi in range(bb):
            sout_ref[bi] = s_scr[bi * n_state:(bi + 1) * n_state]
            mout_ref[bi] = m_scr[bi * n_state:(bi + 1) * n_state]


def _mlstm(qmT, km, vmT, grow, s0, m0):
    B, _, _, T = qmT.shape
    nc = T // CHUNK
    fwd3 = lambda b, c: (b, 0, c, 0)
    bwd3 = lambda b, c: (b, 0, nc - 1 - c, 0)
    fwdT = lambda b, c: (b, 0, 0, c)
    bwdT = lambda b, c: (b, 0, 0, nc - 1 - c)
    bb = MLSTM_BATCH
    rspec = lambda im: pl.BlockSpec((bb, M_HEADS, CHUNK, M_DIM), im)
    tspec = lambda im: pl.BlockSpec((bb, M_HEADS, M_DIM, CHUNK), im)
    n_state = 2 * M_HEADS
    return pl.pallas_call(
        functools.partial(_mlstm_kernel, nc=nc, bb=bb),
        grid=(B // bb, nc),
        in_specs=[
            tspec(fwdT), rspec(fwd3), tspec(fwdT),
            tspec(bwdT), rspec(bwd3), tspec(bwdT),
            pl.BlockSpec((bb, N_GATES, CHUNK), lambda b, c: (b, 0, c)),
            pl.BlockSpec((bb, N_GATES, CHUNK), lambda b, c: (b, 0, nc - 1 - c)),
            pl.BlockSpec((bb, n_state, 2 * M_DIM, M_DIM), lambda b, c: (b, 0, 0, 0)),
            pl.BlockSpec((bb, n_state, 1, LANES), lambda b, c: (b, 0, 0, 0)),
        ],
        out_specs=[
            pl.BlockSpec((bb, CHUNK, M_WIDTH), lambda b, c: (b, c, 0)),
            pl.BlockSpec((bb, CHUNK, M_WIDTH), lambda b, c: (b, nc - 1 - c, 0)),
            pl.BlockSpec((bb, n_state, 2 * M_DIM, M_DIM), lambda b, c: (b, 0, 0, 0)),
            pl.BlockSpec((bb, n_state, 1, LANES), lambda b, c: (b, 0, 0, 0)),
        ],
        out_shape=[
            jax.ShapeDtypeStruct((B, T, M_WIDTH), F32),
            jax.ShapeDtypeStruct((B, T, M_WIDTH), F32),
            jax.ShapeDtypeStruct((B, n_state, 2 * M_DIM, M_DIM), F32),
            jax.ShapeDtypeStruct((B, n_state, 1, LANES), F32),
        ],
        scratch_shapes=[pltpu.VMEM((bb * n_state, 2 * M_DIM, M_DIM), F32),
                        pltpu.VMEM((bb * n_state, 1, LANES), F32)],
        compiler_params=pltpu.CompilerParams(
            dimension_semantics=("parallel", "arbitrary"), vmem_limit_bytes=VMEM_LIMIT),
        name="mlstm",
    )(qmT, km, vmT, qmT, km, vmT, grow, grow, s0, m0)


def _post_kernel(att_ref, hf_ref, hb_ref, so_ref, go_ref, x_ref, mod_ref, wo_ref, l1g_ref, l1b_ref,
                 w1_ref, w2_ref, l2g_ref, l2b_ref, o_ref, *, ff_chunk):
    mod = mod_ref[0]
    g1 = mod[:, 2 * D_MODEL:3 * D_MODEL]
    sh2 = mod[:, 3 * D_MODEL:4 * D_MODEL]
    sc2 = mod[:, 4 * D_MODEL:5 * D_MODEL]
    g2 = mod[:, 5 * D_MODEL:6 * D_MODEL]
    half = att_ref.shape[1] // 2
    rows = [slice(r * half, (r + 1) * half) for r in range(2)]
    ys = []
    for r in rows:
        m_out = (so_ref[0, r].astype(F32) * (hf_ref[0, r] + hb_ref[0, r])).astype(BF16)
        cat = jnp.concatenate([att_ref[0, r], m_out, go_ref[0, r]], axis=1)
        ys.append(_dot(cat, wo_ref[...]))
    for r, y in zip(rows, ys):
        x1 = _layer_norm(DEEPNORM_ALPHA * x_ref[0, r] + g1 * y, l1g_ref[...], l1b_ref[...])
        h2 = (x1 * (1.0 + sc2) + sh2).astype(BF16)
        f = None
        for j in range(D_FF // ff_chunk):
            a = jnp.maximum(_dot(h2, w1_ref[:, j * ff_chunk:(j + 1) * ff_chunk]), 0.0)
            part = _dot((a * a).astype(BF16), w2_ref[j * ff_chunk:(j + 1) * ff_chunk, :])
            f = part if f is None else f + part
        o_ref[0, r] = _layer_norm(DEEPNORM_ALPHA * x1 + g2 * f, l2g_ref[...], l2b_ref[...])


def _post(att, hf, hb, so, go, x, mod, mod_row, wo, l1g, l1b, w1, w2, l2g, l2b, *, tm, ff_chunk):
    B, T, _ = x.shape
    mod_idx = (lambda b, i: (b + mod_row, 0, 0)) if mod_row else (lambda b, i: (0, 0, 0))
    tok = lambda w: pl.BlockSpec((1, tm, w), lambda b, i: (b, i, 0))
    return pl.pallas_call(
        functools.partial(_post_kernel, ff_chunk=ff_chunk),
        grid=(B, T // tm),
        in_specs=[
            tok(ATT_WIDTH), tok(M_WIDTH), tok(M_WIDTH), tok(M_WIDTH), tok(G_WIDTH), tok(D_MODEL),
            pl.BlockSpec((1, 1, 6 * D_MODEL), mod_idx),
            _resident((D_MODEL, D_MODEL)), _resident((1, D_MODEL)), _resident((1, D_MODEL)),
            _resident((D_MODEL, D_FF)), _resident((D_FF, D_MODEL)),
            _resident((1, D_MODEL)), _resident((1, D_MODEL)),
        ],
        out_specs=tok(D_MODEL),
        out_shape=jax.ShapeDtypeStruct((B, T, D_MODEL), F32),
        compiler_params=pltpu.CompilerParams(
            dimension_semantics=("parallel", "parallel"), vmem_limit_bytes=VMEM_LIMIT),
        name="post",
    )(att, hf, hb, so, go, x, mod, wo, l1g, l1b, w1, w2, l2g, l2b)


def _rope_tables(T):
    rows = T // GRID_W
    r, cidx = jnp.meshgrid(jnp.arange(rows, dtype=F32), jnp.arange(GRID_W, dtype=F32), indexing='ij')
    n_freq = HEAD_DIM // 4
    inv = ROPE_BASE ** (-jnp.arange(n_freq, dtype=F32) / n_freq)
    ang_r = (r.reshape(-1)[:, None] * inv[None, :]).T
    ang_c = (cidx.reshape(-1)[:, None] * inv[None, :]).T
    cos = jnp.concatenate([jnp.cos(ang_r)] * 2 + [jnp.cos(ang_c)] * 2, axis=0)
    sin = jnp.concatenate([-jnp.sin(ang_r), jnp.sin(ang_r), -jnp.sin(ang_c), jnp.sin(ang_c)], axis=0)
    return cos, sin


def _with_ones_rows(vT):
    ones = jnp.ones(vT.shape[:-2] + (BF16_SUBLANES, vT.shape[-1]), vT.dtype)
    return jnp.concatenate([vT, ones], axis=-2)


def _pack_state(C, n, m):
    B = C.shape[0]
    ns = 2 * M_HEADS
    CT = jnp.swapaxes(C.reshape(B, ns, M_DIM, M_DIM), -1, -2).astype(F32)
    n = jnp.broadcast_to(n.reshape(B, ns, 1, M_DIM).astype(F32), (B, ns, M_DIM, M_DIM))
    m = jnp.broadcast_to(m.reshape(B, ns, 1, 1).astype(F32), (B, ns, 1, LANES))
    return jnp.concatenate([CT, n], axis=-2), m


def kernel(x_prompt, x_sample, c, cache_attn_k, cache_attn_v, state_mlstm_C, state_mlstm_n, state_mlstm_m,
           c_ctx, w_mod, b_mod, w_in, q_norm_g, k_norm_g, mlstm_gate_b, gmlp_ln_g, gmlp_ln_b, gmlp_ws,
           gmlp_bs, w_out, ln1_g, ln1_b, w_ff1, w_ff2, ln2_g, ln2_b):
    BP, TP, _ = x_prompt.shape
    BS, TS, _ = x_sample.shape
    n_mod = MOD_ROWS
    assert BS + 1 <= n_mod
    cvec = jnp.zeros((n_mod, D_MODEL), F32).at[0].set(c_ctx).at[1:1 + BS].set(c)
    mod_all = _modulation(cvec, w_mod, b_mod).reshape(DEPTH, n_mod, 1, 6 * D_MODEL)

    cos_s, sin_s = _rope_tables(TS)
    tm_p, tm_s = TP, INPROJ_TM

    xp, xs = x_prompt, x_sample
    ks_, vs_, Cs_, ns_, ms_ = [], [], [], [], []
    for l in range(DEPTH):
        wl = w_in[l]
        g0 = OFF_OM + M_WIDTH
        w_in_l = jnp.concatenate(
            [wl[:, :g0], wl[:, g0 + N_GATES:], wl[:, g0:g0 + N_GATES],
             jnp.zeros((D_MODEL, GATE_PAD - N_GATES), F32)], axis=1).astype(BF16)
        gate_b = jnp.zeros((1, GATE_PAD), F32).at[0, :N_GATES].set(mlstm_gate_b[l].reshape(-1))
        lng = gmlp_ln_g[l].reshape(1, G_WIDTH)
        lnb = gmlp_ln_b[l].reshape(1, G_WIDTH)
        ws = gmlp_ws[l].astype(BF16)
        bsx = jnp.repeat(gmlp_bs[l].T, G_CH, axis=1)
        wo = w_out[l].astype(BF16)
        w1 = w_ff1[l].astype(BF16)
        w2 = w_ff2[l].astype(BF16)
        l1g, l1b = ln1_g[l].reshape(1, -1), ln1_b[l].reshape(1, -1)
        l2g, l2b = ln2_g[l].reshape(1, -1), ln2_b[l].reshape(1, -1)
        mod = mod_all[l]

        def gains(tm):
            gq = jnp.broadcast_to((q_norm_g[l] * (HEAD_DIM ** -0.5 * LOG2_E))[:, None], (HEAD_DIM, tm))
            gk = jnp.broadcast_to(k_norm_g[l][:, None], (HEAD_DIM, tm))
            return gq, gk

        gq, gk = gains(tm_p)
        (qT, k, v1T, qmT, km, vmT, so, go, grow, k32, v32) = _inproj(
            xp, mod, 0, w_in_l, gq, gk, None, None, gate_b, lng, lnb, ws, bsx,
            rope=False, emit_kv32=True, tm=tm_p)
        att = _attention(qT, [k], [v1T], tq=TP, kc=TP, hpb=ATT_GROUP)
        zero_state = _pack_state(jnp.zeros((BP, 2, M_HEADS, M_DIM, M_DIM), F32),
                                 jnp.zeros((BP, 2, M_HEADS, M_DIM), F32),
                                 jnp.zeros((BP, 2, M_HEADS), F32))
        hf, hb, s_fin, m_fin = _mlstm(qmT, km, vmT, grow, *zero_state)
        fold = lambda a: a.reshape(BP * TP // POST_TM, POST_TM, a.shape[-1])
        xp = _post(fold(att), fold(hf), fold(hb), fold(so), fold(go), fold(xp), mod, 0, wo, l1g, l1b, w1, w2,
                   l2g, l2b, tm=POST_TM, ff_chunk=1024).reshape(BP, TP, D_MODEL)
        ks_.append(k32.reshape(BP, TP, ATT_KV_HEADS, HEAD_DIM))
        vs_.append(v32.reshape(BP, TP, ATT_KV_HEADS, HEAD_DIM))
        Cs_.append(jnp.swapaxes(s_fin[:, :, :M_DIM, :], -1, -2).reshape(BP, 2, M_HEADS, M_DIM, M_DIM))
        ns_.append(s_fin[:, :, M_DIM, :].reshape(BP, 2, M_HEADS, M_DIM))
        ms_.append(m_fin[:, :, 0, 0].reshape(BP, 2, M_HEADS))

        gq, gk = gains(tm_s)
        (qT, k, v1T, qmT, km, vmT, so, go, grow) = _inproj(
            xs, mod, 1, w_in_l, gq, gk, cos_s, sin_s, gate_b, lng, lnb, ws, bsx,
            rope=True, emit_kv32=False, tm=tm_s)
        k_c = jnp.transpose(cache_attn_k[:, l], (0, 2, 1, 3)).astype(BF16)
        ones_col = jnp.zeros(k_c.shape[:-1] + (LANES - HEAD_DIM,), BF16).at[..., 0].set(1.0)
        k_c = jnp.concatenate([k_c, ones_col], axis=-1)
        vT_c = _with_ones_rows(jnp.transpose(cache_attn_v[:, l], (0, 2, 3, 1)).astype(BF16))
        att = _attention(qT, [k_c, k], [vT_c, v1T], tq=ATT_TQ, kc=ATT_KC, hpb=ATT_HPB)
        state = _pack_state(state_mlstm_C[:, l], state_mlstm_n[:, l], state_mlstm_m[:, l])
        hf, hb, _, _ = _mlstm(qmT, km, vmT, grow, *state)
        xs = _post(att, hf, hb, so, go, xs, mod, 1, wo, l1g, l1b, w1, w2, l2g, l2b, tm=POST_TM, ff_chunk=1024)

    return (xp, xs, jnp.stack(ks_, axis=1), jnp.stack(vs_, axis=1), jnp.stack(Cs_, axis=1),
            jnp.stack(ns_, axis=1), jnp.stack(ms_, axis=1))
```
